```python
import jax, jax.numpy as jnp
from jax import lax
import numpy as np

D_MODEL = 2048
BATCH = 1
SEQ = 16384
DEPTH = 2

CHUNK = 64
EPS = 1e-6

A_HEADS = 8
A_HEAD_DIM = 128
A_WIDTH = A_HEADS * A_HEAD_DIM
A_PAST_CHUNKS = 8
A_BAND = A_PAST_CHUNKS + 1
REL_CLIP = 256
REL_BUCKETS = 2 * REL_CLIP + 1

B_HEADS = 8
B_KEY_DIM = 128
B_VAL_DIM = 128
B_KEY_WIDTH = B_HEADS * B_KEY_DIM
B_VAL_WIDTH = B_HEADS * B_VAL_DIM

C_HEADS = 8
C_QK_DIM = 64
C_V_DIM = 128
C_QK_WIDTH = C_HEADS * C_QK_DIM
C_V_WIDTH = C_HEADS * C_V_DIM
ROPE_BASE = 10000.0

IN_SPLIT_SIZES = (A_WIDTH, A_WIDTH, A_WIDTH,
                  B_KEY_WIDTH, B_KEY_WIDTH, B_VAL_WIDTH, B_VAL_WIDTH,
                  C_QK_WIDTH, C_QK_WIDTH, C_V_WIDTH, C_V_WIDTH)
IN_WIDTH = sum(IN_SPLIT_SIZES)
N_BRANCH = 3
D_FF = 4 * D_MODEL

kernel_name = "hybrid_chunk_stream_block"


def rmsnorm(x, gain=None):
    xf = x.astype(jnp.float32)
    y = xf * lax.rsqrt(jnp.mean(jnp.square(xf), axis=-1, keepdims=True) + EPS)
    if gain is not None:
        y = y * gain.astype(jnp.float32)
    return y.astype(x.dtype)


def split_points():
    return tuple(int(v) for v in np.cumsum(IN_SPLIT_SIZES)[:-1])


def chunked_rel_attention(q, k, v, g_q, g_k, rel_table):
    bsz, seq, _ = q.shape
    n = seq // CHUNK

    def to_chunks(t):
        return t.reshape(bsz, n, CHUNK, A_HEADS, A_HEAD_DIM).transpose(0, 3, 1, 2, 4)

    qc = rmsnorm(to_chunks(q), g_q) * (A_HEAD_DIM ** -0.5)
    kc = rmsnorm(to_chunks(k), g_k)
    vc = to_chunks(v)
    pad = ((0, 0), (0, 0), (A_PAST_CHUNKS, 0), (0, 0), (0, 0))
    kp = jnp.pad(kc, pad)
    vp = jnp.pad(vc, pad)
    scores = jnp.stack(
        [jnp.einsum('bhncd,bhnkd->bhnck', qc, kp[:, :, j:j + n]) for j in range(A_BAND)],
        axis=-2).astype(jnp.float32)
    cq = np.arange(CHUNK)[:, None, None]
    jb = np.arange(A_BAND)[None, :, None]
    ck = np.arange(CHUNK)[None, None, :]
    dist = (A_PAST_CHUNKS - jb) * CHUNK + cq - ck
    idx = np.clip(dist, -REL_CLIP, REL_CLIP) + REL_CLIP
    bias = rel_table.astype(jnp.float32)[:, idx]
    valid = (np.arange(n)[:, None] - A_PAST_CHUNKS + np.arange(A_BAND)[None, :]) >= 0
    scores = jnp.where(valid[None, None, :, None, :, None], scores + bias[None, :, None], -jnp.inf)
    probs = jax.nn.softmax(scores.reshape(bsz, A_HEADS, n, CHUNK, A_BAND * CHUNK), axis=-1)
    probs = probs.reshape(bsz, A_HEADS, n, CHUNK, A_BAND, CHUNK).astype(v.dtype)
    out = sum(jnp.einsum('bhnck,bhnkd->bhncd', probs[:, :, :, :, j], vp[:, :, j:j + n])
              for j in range(A_BAND))
    return out.transpose(0, 2, 3, 1, 4).reshape(bsz, seq, A_WIDTH)


def hgrn2(f_logit, q, i, g, lower_bound, norm_gain):
    bsz, seq, _ = q.shape
    n = seq // CHUNK
    out_dtype = i.dtype
    z = f_logit.astype(jnp.float32)
    lb = lower_bound.astype(jnp.float32)
    log_f = jnp.logaddexp(jnp.log(lb), jnp.log1p(-lb) + jax.nn.log_sigmoid(z))
    k = (1.0 - lb) * jax.nn.sigmoid(-z)
    qf = jax.nn.silu(q.astype(jnp.float32))

    def to_chunks(t, d):
        return t.reshape(bsz, n, CHUNK, B_HEADS, d).transpose(1, 0, 3, 2, 4)

    xs = (to_chunks(qf, B_KEY_DIM), to_chunks(k, B_KEY_DIM),
          to_chunks(i.astype(jnp.float32), B_VAL_DIM), to_chunks(log_f, B_KEY_DIM))
    causal = np.tril(np.ones((CHUNK, CHUNK), dtype=bool))

    def step(state, inp):
        qc, kc, vc, gc = inp
        b = jnp.cumsum(gc, axis=2)
        diff = b[:, :, :, None, :] - b[:, :, None, :, :]
        decay = jnp.exp(jnp.where(causal[:, :, None], diff, -jnp.inf))
        attn = jnp.einsum('bhtd,bhsd,bhtsd->bhts', qc, kc, decay)
        o = (jnp.einsum('bhts,bhse->bhte', attn, vc)
             + jnp.einsum('bhtd,bhde->bhte', qc * jnp.exp(b), state))
        b_last = b[:, :, -1:, :]
        state = (jnp.exp(b_last[:, :, 0, :])[..., None] * state
                 + jnp.einsum('bhsd,bhse->bhde', kc * jnp.exp(b_last - b), vc))
        return state, o

    s0 = jnp.zeros((bsz, B_HEADS, B_KEY_DIM, B_VAL_DIM), jnp.float32)
    _, o = lax.scan(step, s0, xs)
    o = o.transpose(1, 0, 3, 2, 4).reshape(bsz, seq, B_HEADS, B_VAL_DIM)
    gate = jax.nn.silu(g.astype(jnp.float32)).reshape(bsz, seq, B_HEADS, B_VAL_DIM)
    o = rmsnorm(o, norm_gain) * gate
    return o.reshape(bsz, seq, B_VAL_WIDTH).astype(out_dtype)


def rotary(t, pos):
    half = t.shape[-1] // 2
    inv_freq = jnp.asarray(1.0 / ROPE_BASE ** np.linspace(0.0, 1.0, half), jnp.float32)
    ang = pos.astype(jnp.float32)[:, None] * inv_freq[None, :]
    cos, sin = jnp.cos(ang)[:, None, :], jnp.sin(ang)[:, None, :]
    t1, t2 = t[..., :half], t[..., half:]
    return jnp.concatenate([t1 * cos - t2 * sin, t1 * sin + t2 * cos], axis=-1)


def retention(q, k, v, g):
    bsz, seq, _ = q.shape
    n = seq // CHUNK
    out_dtype = v.dtype
    pos = jnp.arange(seq)
    qh = rotary(q.astype(jnp.float32).reshape(bsz, seq, C_HEADS, C_QK_DIM), pos)
    kh = rotary(k.astype(jnp.float32).reshape(bsz, seq, C_HEADS, C_QK_DIM), pos) * (C_QK_DIM ** -0.5)
    vh = v.astype(jnp.float32).reshape(bsz, seq, C_HEADS, C_V_DIM)

    def to_chunks(t):
        return t.reshape(bsz, n, CHUNK, C_HEADS, t.shape[-1]).transpose(0, 3, 1, 2, 4)

    qc, kc, vc = to_chunks(qh), to_chunks(kh), to_chunks(vh)
    log_gamma = jnp.asarray(np.log(1.0 - 2.0 ** (-5.0 - np.arange(C_HEADS))), jnp.float32)
    t = jnp.arange(CHUNK, dtype=jnp.float32)
    rel = t[:, None] - t[None, :]
    intra_decay = jnp.where(rel >= 0, jnp.exp(log_gamma[:, None, None] * jnp.maximum(rel, 0.0)), 0.0)
    scores = jnp.einsum('bhntd,bhnsd->bhnts', qc, kc) * intra_decay[None, :, None]
    o = jnp.einsum('bhnts,bhnse->bhnte', scores, vc)
    k_dec = kc * jnp.exp(log_gamma[:, None] * (CHUNK - 1.0 - t)[None, :])[None, :, None, :, None]
    kv = jnp.einsum('bhnsd,bhnse->nbhde', k_dec, vc)
    chunk_decay = jnp.exp(log_gamma * CHUNK)[None, :, None, None]

    def step(state, kv_n):
        return chunk_decay * state + kv_n, state

    s0 = jnp.zeros((bsz, C_HEADS, C_QK_DIM, C_V_DIM), jnp.float32)
    _, states = lax.scan(step, s0, kv)
    q_dec = qc * jnp.exp(log_gamma[:, None] * (t + 1.0)[None, :])[None, :, None, :, None]
    o = o + jnp.einsum('bhntd,nbhde->bhnte', q_dec, states)
    o = o.transpose(0, 2, 3, 1, 4).reshape(bsz, seq, C_HEADS, C_V_DIM)
    gate = jax.nn.silu(g.astype(jnp.float32)).reshape(bsz, seq, C_HEADS, C_V_DIM)
    o = rmsnorm(o) * gate
    return o.reshape(bsz, seq, C_V_WIDTH).astype(out_dtype)


def setup_inputs(seed: int = 0) -> dict:
    key = jax.random.key(seed)
    ks = jax.random.split(key, 17)
    f32 = jnp.float32

    def normal(k, shape, scale):
        return jax.random.normal(k, shape, f32) * scale

    def gain(k, shape):
        return 1.0 + 0.02 * jax.random.normal(k, shape, f32)

    return {
        "x": normal(ks[0], (BATCH, SEQ, D_MODEL), 1.0),
        "norm_mix_g": gain(ks[1], (DEPTH, D_MODEL)),
        "w_in": normal(ks[2], (DEPTH, D_MODEL, IN_WIDTH), D_MODEL ** -0.5),
        "qn_g": gain(ks[3], (DEPTH, A_HEAD_DIM)),
        "kn_g": gain(ks[4], (DEPTH, A_HEAD_DIM)),
        "rel_bias": normal(ks[5], (DEPTH, A_HEADS, REL_BUCKETS), 0.2),
        "lb_logits": normal(ks[6], (DEPTH, B_KEY_WIDTH), 1.0),
        "hgrn_norm_g": gain(ks[7], (DEPTH, B_VAL_DIM)),
        "w_gate": normal(ks[8], (DEPTH, D_MODEL, N_BRANCH * D_MODEL), D_MODEL ** -0.5),
        "b_gate": normal(ks[9], (DEPTH, N_BRANCH * D_MODEL), 0.02),
        "w_br_a": normal(ks[10], (DEPTH, A_WIDTH, D_MODEL), A_WIDTH ** -0.5),
        "w_br_b": normal(ks[11], (DEPTH, B_VAL_WIDTH, D_MODEL), B_VAL_WIDTH ** -0.5),
        "w_br_c": normal(ks[12], (DEPTH, C_V_WIDTH, D_MODEL), C_V_WIDTH ** -0.5),
        "w_out": normal(ks[13], (DEPTH, D_MODEL, D_MODEL), D_MODEL ** -0.5),
        "norm_ffn_g": gain(ks[14], (DEPTH, D_MODEL)),
        "w_up": normal(ks[15], (DEPTH, D_MODEL, D_FF), D_MODEL ** -0.5),
        "w_down": normal(ks[16], (DEPTH, D_FF, D_MODEL), D_FF ** -0.5),
    }


def reference(x, norm_mix_g, w_in, qn_g, kn_g, rel_bias, lb_logits, hgrn_norm_g, w_gate, b_gate,
              w_br_a, w_br_b, w_br_c, w_out, norm_ffn_g, w_up, w_down):
    lb_cum = jnp.cumsum(jax.nn.softmax(lb_logits.astype(jnp.float32), axis=0), axis=0)
    lower_bounds = lb_cum - lb_cum[0:1]
    points = split_points()
    for l in range(DEPTH):
        h = rmsnorm(x, norm_mix_g[l])
        (aq, ak, av, bf, bq, bi, bg, cq, ck, cv, cg) = jnp.split(h @ w_in[l], points, axis=-1)
        ya = chunked_rel_attention(aq, ak, av, qn_g[l], kn_g[l], rel_bias[l])
        yb = hgrn2(bf, bq, bi, bg, lower_bounds[l], hgrn_norm_g[l])
        yc = retention(cq, ck, cv, cg)
        gates = jax.nn.sigmoid((h @ w_gate[l] + b_gate[l]).astype(jnp.float32)).astype(x.dtype)
        gate_a, gate_b, gate_c = jnp.split(gates, N_BRANCH, axis=-1)
        merged = gate_a * (ya @ w_br_a[l]) + gate_b * (yb @ w_br_b[l]) + gate_c * (yc @ w_br_c[l])
        x = x + merged @ w_out[l]
        h2 = rmsnorm(x, norm_ffn_g[l])
        x = x + jnp.square(jax.nn.relu(h2 @ w_up[l])) @ w_down[l]
    return x
```

```python
import functools

import numpy as np
import jax
import jax.numpy as jnp
from jax import lax
from jax.experimental import pallas as pl
from jax.experimental.pallas import tpu as pltpu

D_MODEL = 2048
CHUNK = 64
EPS = 1e-6

A_HEADS = 8
A_HEAD_DIM = 128
A_WIDTH = A_HEADS * A_HEAD_DIM
A_PAST_CHUNKS = 8
A_BAND = A_PAST_CHUNKS + 1
REL_CLIP = 256

B_HEADS = 8
B_KEY_DIM = 128
B_VAL_DIM = 128
B_WIDTH = B_HEADS * B_KEY_DIM

C_HEADS = 8
C_QK_DIM = 64
C_V_DIM = 128
C_QK_WIDTH = C_HEADS * C_QK_DIM
C_V_WIDTH = C_HEADS * C_V_DIM
ROPE_BASE = 10000.0

N_BRANCH = 3
D_FF = 4 * D_MODEL

LANES = 128
VMEM_LIMIT = 56 * 1024 * 1024

BF16 = jnp.bfloat16
F32 = jnp.float32
NT_DIMS = (((1,), (1,)), ((), ()))
TN_DIMS = (((0,), (0,)), ((), ()))


def _params(*semantics):
    return pltpu.CompilerParams(dimension_semantics=semantics, vmem_limit_bytes=VMEM_LIMIT)


def _rms(t, gain=None):
    y = t * lax.rsqrt(jnp.mean(t * t, axis=-1, keepdims=True) + EPS)
    return y if gain is None else y * gain


def _silu(t):
    return t / (1.0 + jnp.exp(-t))


def _norm_matmul_kernel(x_ref, g_ref, w_ref, b_ref, o_ref, h_ref, *, act):
    @pl.when(pl.program_id(1) == 0)
    def _():
        h_ref[...] = _rms(x_ref[...], g_ref[...]).astype(BF16)

    acc = jnp.dot(h_ref[...], w_ref[...], preferred_element_type=F32) + b_ref[...]
    if act == "sigmoid":
        acc = jax.nn.sigmoid(acc)
    o_ref[...] = acc.astype(o_ref.dtype)


def _norm_matmul(x, g, w, b, *, act, out_dtype, tm, tn, name):
    s, d = x.shape
    n = w.shape[1]
    return pl.pallas_call(
        functools.partial(_norm_matmul_kernel, act=act),
        grid=(s // tm, n // tn),
        in_specs=[
            pl.BlockSpec((tm, d), lambda i, j: (i, 0)),
            pl.BlockSpec((1, d), lambda i, j: (0, 0)),
            pl.BlockSpec((d, tn), lambda i, j: (0, j)),
            pl.BlockSpec((1, tn), lambda i, j: (0, j)),
        ],
        out_specs=pl.BlockSpec((tm, tn), lambda i, j: (i, j)),
        out_shape=jax.ShapeDtypeStruct((s, n), out_dtype),
        scratch_shapes=[pltpu.VMEM((tm, d), BF16)],
        compiler_params=_params("parallel", "arbitrary"),
        name=name,
    )(x, g, w, b)


A_GROUP = A_PAST_CHUNKS * CHUNK
A_WINDOW = A_BAND * CHUNK


def _attn_kernel(q_ref, kp_ref, kc_ref, vp_ref, vc_ref, bias_ref, gq_ref, gk_ref, o_ref,
                 kbuf, vbuf):
    i = pl.program_id(1)
    q = (_rms(q_ref[...], gq_ref[...]) * (A_HEAD_DIM ** -0.5)).astype(BF16)
    kbuf[0:A_GROUP, :] = _rms(kp_ref[...], gk_ref[...]).astype(BF16)
    kbuf[A_GROUP:, :] = _rms(kc_ref[...], gk_ref[...]).astype(BF16)
    vbuf[0:A_GROUP, :] = vp_ref[...]
    vbuf[A_GROUP:, :] = vc_ref[...]
    bias = bias_ref[0]
    col = lax.broadcasted_iota(jnp.int32, (CHUNK, A_WINDOW), 1)
    for a in range(A_GROUP // CHUNK):
        lo = a * CHUNK
        qa = q[lo:lo + CHUNK, :]
        s = lax.dot_general(qa, kbuf[lo:lo + A_WINDOW, :], NT_DIMS, preferred_element_type=F32)
        valid = jnp.logical_or(i > 0, col >= A_GROUP - lo)
        s = jnp.where(valid, s + bias, -jnp.inf)
        m = jnp.max(s, axis=-1, keepdims=True)
        p = jnp.exp(s - m)
        l = jnp.sum(p, axis=-1, keepdims=True)
        o = jnp.dot(p.astype(BF16), vbuf[lo:lo + A_WINDOW, :], preferred_element_type=F32)
        o_ref[lo:lo + CHUNK, :] = (o / l).astype(o_ref.dtype)


def _attention(pf, pb, bias, gq, gk, *, q_col, k_col, v_col):
    s = pf.shape[0]
    blk = (A_GROUP, A_HEAD_DIM)
    prev = lambda c: (lambda h, i: (jnp.maximum(i - 1, 0), c + h))
    cur = lambda c: (lambda h, i: (i, c + h))
    return pl.pallas_call(
        _attn_kernel,
        grid=(A_HEADS, s // A_GROUP),
        in_specs=[
            pl.BlockSpec(blk, cur(q_col)),
            pl.BlockSpec(blk, prev(k_col)),
            pl.BlockSpec(blk, cur(k_col)),
            pl.BlockSpec(blk, prev(v_col)),
            pl.BlockSpec(blk, cur(v_col)),
            pl.BlockSpec((1, CHUNK, A_WINDOW), lambda h, i: (h, 0, 0)),
            pl.BlockSpec((1, A_HEAD_DIM), lambda h, i: (0, 0)),
            pl.BlockSpec((1, A_HEAD_DIM), lambda h, i: (0, 0)),
        ],
        out_specs=pl.BlockSpec(blk, lambda h, i: (i, h)),
        out_shape=jax.ShapeDtypeStruct((s, A_WIDTH), BF16),
        scratch_shapes=[pltpu.VMEM((2 * A_GROUP, A_HEAD_DIM), BF16),
                        pltpu.VMEM((2 * A_GROUP, A_HEAD_DIM), BF16)],
        compiler_params=_params("parallel", "arbitrary"),
        name="attention",
    )(pf, pf, pf, pb, pb, bias, gq, gk)


def _rel_bias_window(rel_table):
    cq = np.arange(CHUNK)[:, None]
    kpos = np.arange(A_WINDOW)[None, :] - A_PAST_CHUNKS * CHUNK
    idx = np.clip(cq - kpos, -REL_CLIP, REL_CLIP) + REL_CLIP
    return rel_table.astype(F32)[:, idx]


B_LEVELS = (32, 16, 8, 4, 2, 1)


def _pair_level_matrix():
    t = np.arange(CHUNK)[:, None]
    s = np.arange(CHUNK)[None, :]
    lvl = np.full((CHUNK, CHUNK), -1, np.int32)
    lvl[t == s] = 0
    for n, h in enumerate(B_LEVELS):
        m = (t // (2 * h) == s // (2 * h)) & (t % (2 * h) >= h) & (s % (2 * h) < h)
        lvl[m] = n + 1
    return lvl


def _hgrn_chunk(z, qin, v, g, loglb, log1mlb, one_m_lb, gain, level, state_t):
    row = lax.broadcasted_iota(jnp.int32, (CHUNK, B_KEY_DIM), 0)
    e = jnp.exp(-jnp.abs(z))
    log_sig = jnp.minimum(z, 0.0) - jnp.log1p(e)
    bb = log1mlb + log_sig
    lf = jnp.maximum(loglb, bb) + jnp.log1p(jnp.exp(-jnp.abs(loglb - bb)))
    kk = one_m_lb * (jnp.where(z >= 0.0, e, 1.0) / (1.0 + e))
    qf = _silu(qin)

    b = lf
    for sh in (1, 2, 4, 8, 16, 32):
        b = b + jnp.where(row >= sh, pltpu.roll(b, sh, 0), 0.0)

    def pairs(expo):
        w = jnp.exp(expo)
        return lax.dot_general((qf * w).astype(BF16), (kk * w).astype(BF16), NT_DIMS,
                               preferred_element_type=F32)

    attn = jnp.where(level == 0, pairs(jnp.zeros_like(b)), 0.0)
    for n, h in enumerate(B_LEVELS):
        if h >= 4:
            bound = jnp.concatenate(
                [jnp.broadcast_to(b[blk + h - 1:blk + h, :], (2 * h, B_KEY_DIM))
                 for blk in range(0, CHUNK, 2 * h)], axis=0)
            expo = -jnp.abs(b - bound)
        elif h == 2:
            r4 = row % 4
            expo = jnp.where(r4 == 0, pltpu.roll(lf, CHUNK - 1, 0),
                             jnp.where(r4 == 1, 0.0,
                                       jnp.where(r4 == 2, lf, lf + pltpu.roll(lf, 1, 0))))
        else:
            expo = jnp.where(row % 2 == 1, lf, 0.0)
        attn = jnp.where(level == n + 1, pairs(expo), attn)

    b_last = b[CHUNK - 1:CHUNK, :]
    o = jnp.dot(attn.astype(BF16), v, preferred_element_type=F32)
    o = o + lax.dot_general((qf * jnp.exp(b)).astype(BF16), state_t.astype(BF16), NT_DIMS,
                            preferred_element_type=F32)
    k_dec = (kk * jnp.exp(b_last - b)).astype(BF16)
    new_state_t = jnp.exp(b_last) * state_t + lax.dot_general(v, k_dec, TN_DIMS,
                                                               preferred_element_type=F32)
    return _rms(o, gain) * _silu(g), new_state_t


def _hgrn_kernel(z_ref, q_ref, v_ref, g_ref, loglb_ref, log1mlb_ref, omlb_ref, gain_ref, level_ref,
                 o_ref, state_ref, *, n_chunks):
    @pl.when(pl.program_id(0) == 0)
    def _():
        state_ref[...] = jnp.zeros_like(state_ref)

    level = level_ref[...]
    gain = gain_ref[...]

    def chunk_body(c, carry):
        rows = pl.ds(pl.multiple_of(c * CHUNK, CHUNK), CHUNK)
        for h in range(B_HEADS):
            cols = slice(h * B_KEY_DIM, (h + 1) * B_KEY_DIM)
            y, st = _hgrn_chunk(z_ref[rows, cols], q_ref[rows, cols], v_ref[rows, cols], g_ref[rows, cols],
                                loglb_ref[:, cols], log1mlb_ref[:, cols], omlb_ref[:, cols], gain, level,
                                state_ref[h])
            state_ref[h] = st
            o_ref[rows, cols] = y.astype(o_ref.dtype)
        return carry

    lax.fori_loop(0, n_chunks, chunk_body, 0)


def _hgrn(pf, pb, lower_bound, gain, *, z_col, q_col, g_col, v_col, tb):
    s = pf.shape[0]
    lb = lower_bound.astype(F32).reshape(1, B_WIDTH)
    consts = [jnp.log(lb), jnp.log1p(-lb), 1.0 - lb]
    level = jnp.asarray(_pair_level_matrix())
    wide = lambda c: pl.BlockSpec((tb, B_WIDTH), lambda i: (i, c))
    row = pl.BlockSpec((1, B_WIDTH), lambda i: (0, 0))
    return pl.pallas_call(
        functools.partial(_hgrn_kernel, n_chunks=tb // CHUNK),
        grid=(s // tb,),
        in_specs=[wide(z_col), wide(q_col), wide(v_col), wide(g_col), row, row, row,
                  pl.BlockSpec((1, B_VAL_DIM), lambda i: (0, 0)),
                  pl.BlockSpec((CHUNK, CHUNK), lambda i: (0, 0))],
        out_specs=pl.BlockSpec((tb, B_WIDTH), lambda i: (i, 0)),
        out_shape=jax.ShapeDtypeStruct((s, B_WIDTH), BF16),
        scratch_shapes=[pltpu.VMEM((B_HEADS, B_VAL_DIM, B_KEY_DIM), F32)],
        compiler_params=_params("arbitrary"),
        name="hgrn2",
    )(pf, pf, pb, pf, *consts, gain, level)


C_BLOCK = 256


def _retention_kernel(q_ref, k_ref, v_ref, g_ref, cos_ref, sin_ref, o_ref, state_ref):
    @pl.when(pl.program_id(0) == 0)
    def _():
        state_ref[...] = jnp.zeros_like(state_ref)

    t = C_BLOCK
    cos = cos_ref[...]
    sin = sin_ref[...]
    lane = lax.broadcasted_iota(jnp.int32, (t, LANES), 1)
    first_half = (lane % C_QK_DIM) < (C_QK_DIM // 2)

    def rotary(x):
        swapped = jnp.where(first_half, pltpu.roll(x, LANES - C_QK_DIM // 2, 1),
                            pltpu.roll(x, C_QK_DIM // 2, 1))
        return x * cos + swapped * sin

    ti = lax.broadcasted_iota(jnp.int32, (t, t), 0)
    si = lax.broadcasted_iota(jnp.int32, (t, t), 1)
    rel = (ti - si).astype(F32)
    pos = lax.broadcasted_iota(jnp.int32, (t, C_QK_DIM), 0).astype(F32)

    for pair in range(C_HEADS // 2):
        lanes = slice(pair * LANES, (pair + 1) * LANES)
        q2 = rotary(q_ref[:, lanes])
        k2 = rotary(k_ref[:, lanes]) * (C_QK_DIM ** -0.5)
        for sub in range(2):
            h = 2 * pair + sub
            log_gamma = np.float32(np.log(1.0 - 2.0 ** (-5.0 - h)))
            qh = q2[:, sub * C_QK_DIM:(sub + 1) * C_QK_DIM]
            kh = k2[:, sub * C_QK_DIM:(sub + 1) * C_QK_DIM]
            vh = v_ref[:, h * C_V_DIM:(h + 1) * C_V_DIM]
            decay = jnp.where(rel >= 0.0, jnp.exp(log_gamma * jnp.maximum(rel, 0.0)), 0.0)
            scores = lax.dot_general(qh.astype(BF16), kh.astype(BF16), NT_DIMS,
                                     preferred_element_type=F32) * decay
            o = jnp.dot(scores.astype(BF16), vh, preferred_element_type=F32)
            state = state_ref[h]
            q_dec = qh * jnp.exp(log_gamma * (pos + 1.0))
            o = o + jnp.dot(q_dec.astype(BF16), state.astype(BF16), preferred_element_type=F32)
            k_dec = kh * jnp.exp(log_gamma * (t - 1.0 - pos))
            state_ref[h] = (np.float32(np.exp(log_gamma * t)) * state
                            + lax.dot_general(k_dec.astype(BF16), vh, TN_DIMS, preferred_element_type=F32))
            gate = _silu(g_ref[:, h * C_V_DIM:(h + 1) * C_V_DIM])
            o_ref[:, h * C_V_DIM:(h + 1) * C_V_DIM] = (_rms(o) * gate).astype(o_ref.dtype)


def _rotary_tables(seq):
    half = C_QK_DIM // 2
    inv_freq = jnp.asarray(1.0 / ROPE_BASE ** np.linspace(0.0, 1.0, half), F32)
    ang = jnp.arange(seq).astype(F32)[:, None] * inv_freq[None, :]
    cos, sin = jnp.cos(ang), jnp.sin(ang)
    reps = LANES // C_QK_DIM
    return (jnp.tile(jnp.concatenate([cos, cos], axis=1), (1, reps)),
            jnp.tile(jnp.concatenate([-sin, sin], axis=1), (1, reps)))


def _retention(pf, pb, cos, sin, *, q_col, k_col, g_col, v_col):
    s = pf.shape[0]
    t = C_BLOCK
    return pl.pallas_call(
        _retention_kernel,
        grid=(s // t,),
        in_specs=[
            pl.BlockSpec((t, C_QK_WIDTH), lambda i: (i, q_col)),
            pl.BlockSpec((t, C_QK_WIDTH), lambda i: (i, k_col)),
            pl.BlockSpec((t, C_V_WIDTH), lambda i: (i, v_col)),
            pl.BlockSpec((t, C_V_WIDTH), lambda i: (i, g_col)),
            pl.BlockSpec((t, LANES), lambda i: (i, 0)),
            pl.BlockSpec((t, LANES), lambda i: (i, 0)),
        ],
        out_specs=pl.BlockSpec((t, C_V_WIDTH), lambda i: (i, 0)),
        out_shape=jax.ShapeDtypeStruct((s, C_V_WIDTH), BF16),
        scratch_shapes=[pltpu.VMEM((C_HEADS, C_QK_DIM, C_V_DIM), F32)],
        compiler_params=_params("arbitrary"),
        name="retention",
    )(pf, pf, pb, pf, cos, sin)


def _merge_kernel(ya_ref, yb_ref, yc_ref, ga_ref, gb_ref, gc_ref, x_ref, wa_ref, wb_ref, wc_ref,
                  wo_ref, gn_ref, xo_ref, h_ref):
    merged = ga_ref[...] * jnp.dot(ya_ref[...], wa_ref[...], preferred_element_type=F32)
    merged = merged + gb_ref[...] * jnp.dot(yb_ref[...], wb_ref[...], preferred_element_type=F32)
    merged = merged + gc_ref[...] * jnp.dot(yc_ref[...], wc_ref[...], preferred_element_type=F32)
    x_new = x_ref[...] + jnp.dot(merged.astype(BF16), wo_ref[...], preferred_element_type=F32)
    xo_ref[...] = x_new
    h_ref[...] = _rms(x_new, gn_ref[...]).astype(BF16)


def _merge_out(ya, yb, yc, gates, x, wa, wb, wc, wo, gn, *, tm):
    s, d = x.shape
    once = pl.Buffered(1)
    y_spec = pl.BlockSpec((tm, A_WIDTH), lambda i: (i, 0))
    gate_spec = lambda c: pl.BlockSpec((tm, d), lambda i: (i, c))
    w_spec = pl.BlockSpec((A_WIDTH, d), lambda i: (0, 0), pipeline_mode=once)
    return pl.pallas_call(
        _merge_kernel,
        grid=(s // tm,),
        in_specs=[y_spec, y_spec, y_spec, gate_spec(0), gate_spec(1), gate_spec(2),
                  pl.BlockSpec((tm, d), lambda i: (i, 0)),
                  w_spec, w_spec, w_spec,
                  pl.BlockSpec((d, d), lambda i: (0, 0), pipeline_mode=once),
                  pl.BlockSpec((1, d), lambda i: (0, 0))],
        out_specs=[pl.BlockSpec((tm, d), lambda i: (i, 0)), pl.BlockSpec((tm, d), lambda i: (i, 0))],
        out_shape=[jax.ShapeDtypeStruct((s, d), F32), jax.ShapeDtypeStruct((s, d), BF16)],
        compiler_params=_params("parallel"),
        name="merge_out",
    )(ya, yb, yc, gates, gates, gates, x, wa, wb, wc, wo, gn)


def _mlp_kernel(h_ref, x_ref, wu_ref, wd_ref, o_ref):
    @pl.when(pl.program_id(1) == 0)
    def _():
        o_ref[...] = x_ref[...]

    up = jnp.maximum(jnp.dot(h_ref[...], wu_ref[...], preferred_element_type=F32), 0.0)
    o_ref[...] += jnp.dot((up * up).astype(BF16), wd_ref[...], preferred_element_type=F32)


def _mlp(h2, x, wu, wd, *, tm, tf):
    s, d = x.shape
    ff = wu.shape[1]
    return pl.pallas_call(
        _mlp_kernel,
        grid=(s // tm, ff // tf),
        in_specs=[pl.BlockSpec((tm, d), lambda i, f: (i, 0)),
                  pl.BlockSpec((tm, d), lambda i, f: (i, 0)),
                  pl.BlockSpec((d, tf), lambda i, f: (0, f)),
                  pl.BlockSpec((tf, d), lambda i, f: (f, 0))],
        out_specs=pl.BlockSpec((tm, d), lambda i, f: (i, 0)),
        out_shape=jax.ShapeDtypeStruct((s, d), F32),
        compiler_params=_params("parallel", "arbitrary"),
        name="mlp",
    )(h2, x, wu, wd)


def _split_w_in(w):
    sizes = (A_WIDTH, A_WIDTH, A_WIDTH, B_WIDTH, B_WIDTH, B_WIDTH, B_WIDTH,
             C_QK_WIDTH, C_QK_WIDTH, C_V_WIDTH, C_V_WIDTH)
    aq, ak, av, bf, bq, bi, bg, cq, ck, cv, cg = jnp.split(w, np.cumsum(sizes)[:-1], axis=1)
    return (jnp.concatenate([aq, ak, bf, bq, bg, cq, ck, cg], axis=1),
            jnp.concatenate([av, bi, cv], axis=1))


def _layer(x, p, lower_bound, cos, sin, *, tm_proj, tm_merge, tm_mlp, tf_mlp, tb_hgrn):
    (norm_mix_g, w_in, qn_g, kn_g, rel_bias, hgrn_norm_g, w_gate, b_gate,
     w_br_a, w_br_b, w_br_c, w_out, norm_ffn_g, w_up, w_down) = p
    w_f, w_b = _split_w_in(w_in)
    g = norm_mix_g.reshape(1, D_MODEL)
    zeros = lambda w: jnp.zeros((1, w.shape[1]), F32)
    pf = _norm_matmul(x, g, w_f.astype(BF16), zeros(w_f), act=None, out_dtype=F32,
                      tm=tm_proj, tn=1024, name="proj_f32")
    pb = _norm_matmul(x, g, w_b.astype(BF16), zeros(w_b), act=None, out_dtype=BF16,
                      tm=tm_proj, tn=1024, name="proj_bf16")
    gates = _norm_matmul(x, g, w_gate.astype(BF16), b_gate.reshape(1, -1), act="sigmoid", out_dtype=F32,
                         tm=tm_proj, tn=1024, name="proj_gates")

    ya = _attention(pf, pb, _rel_bias_window(rel_bias), qn_g.reshape(1, -1), kn_g.reshape(1, -1),
                    q_col=0, k_col=A_WIDTH // A_HEAD_DIM, v_col=0)
    yb = _hgrn(pf, pb, lower_bound, hgrn_norm_g.reshape(1, -1),
               z_col=2, q_col=3, g_col=4, v_col=1, tb=tb_hgrn)
    yc = _retention(pf, pb, cos, sin, q_col=5120 // C_QK_WIDTH, k_col=5632 // C_QK_WIDTH,
                    g_col=6144 // C_V_WIDTH, v_col=2)

    x_new, h2 = _merge_out(ya, yb, yc, gates, x, w_br_a.astype(BF16), w_br_b.astype(BF16),
                           w_br_c.astype(BF16), w_out.astype(BF16), norm_ffn_g.reshape(1, -1),
                           tm=tm_merge)
    return _mlp(h2, x_new, w_up.astype(BF16), w_down.astype(BF16), tm=tm_mlp, tf=tf_mlp)


def _block(x, norm_mix_g, w_in, qn_g, kn_g, rel_bias, lb_logits, hgrn_norm_g, w_gate, b_gate,
           w_br_a, w_br_b, w_br_c, w_out, norm_ffn_g, w_up, w_down, **tiles):
    bsz, seq, d = x.shape
    depth = w_in.shape[0]
    lb_cum = jnp.cumsum(jax.nn.softmax(lb_logits.astype(F32), axis=0), axis=0)
    lower_bounds = lb_cum - lb_cum[0:1]
    cos, sin = _rotary_tables(seq)
    outs = []
    for b in range(bsz):
        xb = x[b]
        for l in range(depth):
            p = (norm_mix_g[l], w_in[l], qn_g[l], kn_g[l], rel_bias[l], hgrn_norm_g[l], w_gate[l],
                 b_gate[l], w_br_a[l], w_br_b[l], w_br_c[l], w_out[l], norm_ffn_g[l], w_up[l], w_down[l])
            xb = _layer(xb, p, lower_bounds[l], cos, sin, **tiles)
        outs.append(xb)
    return jnp.stack(outs, axis=0)


def kernel(x, norm_mix_g, w_in, qn_g, kn_g, rel_bias, lb_logits, hgrn_norm_g, w_gate, b_gate,
           w_br_a, w_br_b, w_br_c, w_out, norm_ffn_g, w_up, w_down):
    return _block(x, norm_mix_g, w_in, qn_g, kn_g, rel_bias, lb_logits, hgrn_norm_g, w_gate, b_gate,
                  w_br_a, w_br_b, w_br_c, w_out, norm_ffn_g, w_up, w_down,
                  tm_proj=1024, tm_merge=256, tm_mlp=512, tf_mlp=512, tb_hgrn=256)
```

```python
import functools

import numpy as np
import jax
import jax.numpy as jnp
from jax import lax
from jax.experimental import pallas as pl
from jax.experimental.pallas import tpu as pltpu

D_MODEL = 2048
CHUNK = 64
EPS = 1e-6

A_HEADS = 8
A_HEAD_DIM = 128
A_WIDTH = A_HEADS * A_HEAD_DIM
A_PAST_CHUNKS = 8
A_BAND = A_PAST_CHUNKS + 1
REL_CLIP = 256

B_HEADS = 8
B_KEY_DIM = 128
B_VAL_DIM = 128
B_WIDTH = B_HEADS * B_KEY_DIM

C_HEADS = 8
C_QK_DIM = 64
C_V_DIM = 128
C_QK_WIDTH = C_HEADS * C_QK_DIM
C_V_WIDTH = C_HEADS * C_V_DIM
ROPE_BASE = 10000.0

N_BRANCH = 3
D_FF = 4 * D_MODEL

LANES = 128
VMEM_LIMIT = 56 * 1024 * 1024

BF16 = jnp.bfloat16
F32 = jnp.float32
NT_DIMS = (((1,), (1,)), ((), ()))
TN_DIMS = (((0,), (0,)), ((), ()))


def _params(*semantics):
    return pltpu.CompilerParams(dimension_semantics=semantics, vmem_limit_bytes=VMEM_LIMIT)


def _rms(t, gain=None):
    y = t * lax.rsqrt(jnp.mean(t * t, axis=-1, keepdims=True) + EPS)
    return y if gain is None else y * gain


def _silu(t):
    return t / (1.0 + jnp.exp(-t))


def _norm_matmul_kernel(x_ref, g_ref, w_ref, b_ref, o_ref, h_ref, *, act):
    @pl.when(pl.program_id(1) == 0)
    def _():
        h_ref[...] = _rms(x_ref[...], g_ref[...]).astype(BF16)

    acc = jnp.dot(h_ref[...], w_ref[...], preferred_element_type=F32) + b_ref[...]
    if act == "sigmoid":
        acc = jax.nn.sigmoid(acc)
    o_ref[...] = acc.astype(o_ref.dtype)


def _norm_matmul(x, g, w, b, *, act, out_dtype, tm, tn, name):
    s, d = x.shape
    n = w.shape[1]
    return pl.pallas_call(
        functools.partial(_norm_matmul_kernel, act=act),
        grid=(s // tm, n // tn),
        in_specs=[
            pl.BlockSpec((tm, d), lambda i, j: (i, 0)),
            pl.BlockSpec((1, d), lambda i, j: (0, 0)),
            pl.BlockSpec((d, tn), lambda i, j: (0, j)),
            pl.BlockSpec((1, tn), lambda i, j: (0, j)),
        ],
        out_specs=pl.BlockSpec((tm, tn), lambda i, j: (i, j)),
        out_shape=jax.ShapeDtypeStruct((s, n), out_dtype),
        scratch_shapes=[pltpu.VMEM((tm, d), BF16)],
        compiler_params=_params("parallel", "arbitrary"),
        name=name,
    )(x, g, w, b)


A_GROUP = A_PAST_CHUNKS * CHUNK
A_SUB = 4 * CHUNK
A_SUB_WINDOW = A_SUB + A_PAST_CHUNKS * CHUNK


def _attn_kernel(q_ref, k_ref, v_ref, bias_ref, gq_ref, gk_ref, o_ref, kbuf, vbuf):
    i = pl.program_id(1)

    @pl.when(i == 0)
    def _():
        kbuf[0:A_GROUP, :] = jnp.zeros((A_GROUP, A_HEAD_DIM), BF16)
        vbuf[0:A_GROUP, :] = jnp.zeros((A_GROUP, A_HEAD_DIM), BF16)

    @pl.when(i > 0)
    def _():
        kbuf[0:A_GROUP, :] = kbuf[A_GROUP:, :]
        vbuf[0:A_GROUP, :] = vbuf[A_GROUP:, :]

    kbuf[A_GROUP:, :] = _rms(k_ref[...], gk_ref[...]).astype(BF16)
    vbuf[A_GROUP:, :] = v_ref[...]
    q = (_rms(q_ref[...], gq_ref[...]) * (A_HEAD_DIM ** -0.5)).astype(BF16)

    def attend(first_block):
        for lo in range(0, A_GROUP, A_SUB):
            s = lax.dot_general(q[lo:lo + A_SUB, :], kbuf[lo:lo + A_SUB_WINDOW, :], NT_DIMS,
                                preferred_element_type=F32) + bias_ref[0]
            if first_block:
                col = lax.broadcasted_iota(jnp.int32, s.shape, 1)
                s = jnp.where(col >= A_GROUP - lo, s, -jnp.inf)
            m = jnp.max(s, axis=-1, keepdims=True)
            p = jnp.exp(s - m)
            l = jnp.sum(p, axis=-1, keepdims=True)
            o = jnp.dot(p.astype(BF16), vbuf[lo:lo + A_SUB_WINDOW, :], preferred_element_type=F32)
            o_ref[lo:lo + A_SUB, :] = (o / l).astype(o_ref.dtype)

    pl.when(i == 0)(functools.partial(attend, True))
    pl.when(i > 0)(functools.partial(attend, False))


def _attention(pf, pb, bias, gq, gk, *, q_col, k_col, v_col):
    s = pf.shape[0]
    blk = (A_GROUP, A_HEAD_DIM)
    cur = lambda c: (lambda h, i: (i, c + h))
    return pl.pallas_call(
        _attn_kernel,
        grid=(A_HEADS, s // A_GROUP),
        in_specs=[
            pl.BlockSpec(blk, cur(q_col)),
            pl.BlockSpec(blk, cur(k_col)),
            pl.BlockSpec(blk, cur(v_col)),
            pl.BlockSpec((1, A_SUB, A_SUB_WINDOW), lambda h, i: (h, 0, 0)),
            pl.BlockSpec((1, A_HEAD_DIM), lambda h, i: (0, 0)),
            pl.BlockSpec((1, A_HEAD_DIM), lambda h, i: (0, 0)),
        ],
        out_specs=pl.BlockSpec(blk, lambda h, i: (i, h)),
        out_shape=jax.ShapeDtypeStruct((s, A_WIDTH), BF16),
        scratch_shapes=[pltpu.VMEM((2 * A_GROUP, A_HEAD_DIM), BF16),
                        pltpu.VMEM((2 * A_GROUP, A_HEAD_DIM), BF16)],
        compiler_params=_params("arbitrary", "arbitrary"),
        name="attention",
    )(pf, pf, pb, bias, gq, gk)


def _rel_bias_window(rel_table):
    t = rel_table.astype(F32)
    n_heads = t.shape[0]
    past = A_PAST_CHUNKS * CHUNK
    n_far = past + A_SUB - 1 - REL_CLIP
    p = jnp.concatenate([jnp.broadcast_to(t[:, 2 * REL_CLIP:], (n_heads, n_far)),
                         t[:, ::-1][:, :A_SUB + A_SUB_WINDOW - 1 - n_far]], axis=1)
    width = A_SUB_WINDOW + A_SUB - CHUNK
    first = jnp.stack([p[:, CHUNK - 1 - r:CHUNK - 1 - r + width] for r in range(CHUNK)], axis=1)
    bias = jnp.concatenate([first[:, :, A_SUB - CHUNK - lo:A_SUB - CHUNK - lo + A_SUB_WINDOW]
                            for lo in range(0, A_SUB, CHUNK)], axis=1)
    q_chunk = np.arange(A_SUB)[:, None] // CHUNK
    k_chunk = np.arange(A_SUB_WINDOW)[None, :] // CHUNK
    band = (k_chunk >= q_chunk) & (k_chunk <= q_chunk + A_PAST_CHUNKS)
    return jnp.where(band[None], bias, -jnp.inf)


B_LEVELS = (32, 16, 8, 4, 2, 1)


def _pair_level_matrix():
    t = np.arange(CHUNK)[:, None]
    s = np.arange(CHUNK)[None, :]
    lvl = np.full((CHUNK, CHUNK), -1, np.int32)
    lvl[t == s] = 0
    for n, h in enumerate(B_LEVELS):
        m = (t // (2 * h) == s // (2 * h)) & (t % (2 * h) >= h) & (s % (2 * h) < h)
        lvl[m] = n + 1
    return lvl


LOG2E = float(np.log2(np.e))


def _neg_abs(t):
    bits = lax.bitcast_convert_type(t, jnp.uint32) | jnp.uint32(0x80000000)
    return lax.bitcast_convert_type(bits, F32)


def _hgrn_chunk(z, qin, v, log2lb, log2_1mlb, one_m_lb, level, state_t):
    row = lax.broadcasted_iota(jnp.int32, (CHUNK, B_KEY_DIM), 0)
    z2 = z * LOG2E
    e = jnp.exp2(_neg_abs(z2))
    log2_sig = jnp.minimum(z2, 0.0) - jnp.log2(1.0 + e)
    bb = log2_1mlb + log2_sig
    lf = jnp.maximum(log2lb, bb) + jnp.log2(1.0 + jnp.exp2(_neg_abs(log2lb - bb)))
    kb = (one_m_lb * (jnp.where(z >= 0.0, e, 1.0) / (1.0 + e))).astype(BF16)
    qb = _silu(qin).astype(BF16)

    b = lf
    for sh in (1, 2, 4):
        b = b + jnp.where(row >= sh, pltpu.roll(b, sh, 0), 0.0)
    for sh in (8, 16, 32):
        b = b + jnp.concatenate([jnp.zeros((sh, B_KEY_DIM), F32), b[:CHUNK - sh, :]], axis=0)

    def pairs(qw, kw):
        return lax.dot_general(qw, kw, NT_DIMS, preferred_element_type=F32)

    attn = jnp.where(level == 0, pairs(qb, kb), 0.0)
    for n, h in enumerate(B_LEVELS):
        if h >= 4:
            bound = jnp.concatenate(
                [jnp.broadcast_to(b[blk + h - 1:blk + h, :], (2 * h, B_KEY_DIM))
                 for blk in range(0, CHUNK, 2 * h)], axis=0)
            expo = _neg_abs(b - bound)
        elif h == 2:
            r4 = row % 4
            expo = jnp.where(r4 == 0, pltpu.roll(lf, CHUNK - 1, 0),
                             jnp.where(r4 == 1, 0.0,
                                       jnp.where(r4 == 2, lf, lf + pltpu.roll(lf, 1, 0))))
        else:
            expo = jnp.where(row % 2 == 1, lf, 0.0)
        w = jnp.exp2(expo).astype(BF16)
        attn = jnp.where(level == n + 1, pairs(qb * w, kb * w), attn)

    b_last = b[CHUNK - 1:CHUNK, :]
    o_past = lax.dot_general(qb * jnp.exp2(b).astype(BF16), state_t.astype(BF16), NT_DIMS,
                             preferred_element_type=F32)
    k_dec = kb * jnp.exp2(b_last - b).astype(BF16)
    new_state_t = jnp.exp2(b_last) * state_t + lax.dot_general(v, k_dec, TN_DIMS,
                                                                preferred_element_type=F32)
    return attn.astype(BF16), o_past, new_state_t


def _hgrn_chunk_out(attn, o_past, v, g, gain):
    o = o_past + jnp.dot(attn, v, preferred_element_type=F32)
    return _rms(o, gain) * _silu(g)


def _hgrn_kernel(z_ref, q_ref, v_ref, g_ref, log2lb_ref, log2_1mlb_ref, omlb_ref, gain_ref, level_ref,
                 o_ref, state_ref, *, n_chunks):
    @pl.when(pl.program_id(0) == 0)
    def _():
        state_ref[...] = jnp.zeros_like(state_ref)

    level = level_ref[...]
    gain = gain_ref[...]

    def chunk_body(c, carry):
        rows = pl.ds(pl.multiple_of(c * CHUNK, CHUNK), CHUNK)
        head_cols = [slice(h * B_KEY_DIM, (h + 1) * B_KEY_DIM) for h in range(B_HEADS)]
        partial_out = []
        for h, cols in enumerate(head_cols):
            attn, o_past, st = _hgrn_chunk(z_ref[rows, cols], q_ref[rows, cols], v_ref[rows, cols],
                                           log2lb_ref[:, cols], log2_1mlb_ref[:, cols], omlb_ref[:, cols],
                                           level, state_ref[h])
            state_ref[h] = st
            partial_out.append((attn, o_past))
        for (attn, o_past), cols in zip(partial_out, head_cols):
            y = _hgrn_chunk_out(attn, o_past, v_ref[rows, cols], g_ref[rows, cols], gain)
            o_ref[rows, cols] = y.astype(o_ref.dtype)
        return carry

    lax.fori_loop(0, n_chunks, chunk_body, 0)


def _hgrn(pf, pb, lower_bound, gain, *, z_col, q_col, g_col, v_col, tb):
    s = pf.shape[0]
    lb = lower_bound.astype(F32).reshape(1, B_WIDTH)
    consts = [jnp.log2(lb), jnp.log1p(-lb) * LOG2E, 1.0 - lb]
    level = jnp.asarray(_pair_level_matrix())
    wide = lambda c: pl.BlockSpec((tb, B_WIDTH), lambda i: (i, c))
    row = pl.BlockSpec((1, B_WIDTH), lambda i: (0, 0))
    return pl.pallas_call(
        functools.partial(_hgrn_kernel, n_chunks=tb // CHUNK),
        grid=(s // tb,),
        in_specs=[wide(z_col), wide(q_col), wide(v_col), wide(g_col), row, row, row,
                  pl.BlockSpec((1, B_VAL_DIM), lambda i: (0, 0)),
                  pl.BlockSpec((CHUNK, CHUNK), lambda i: (0, 0))],
        out_specs=pl.BlockSpec((tb, B_WIDTH), lambda i: (i, 0)),
        out_shape=jax.ShapeDtypeStruct((s, B_WIDTH), BF16),
        scratch_shapes=[pltpu.VMEM((B_HEADS, B_VAL_DIM, B_KEY_DIM), F32)],
        compiler_params=_params("arbitrary"),
        name="hgrn2",
    )(pf, pf, pb, pf, *consts, gain, level)


C_BLOCK = 256


def _retention_kernel(q_ref, k_ref, v_ref, g_ref, cos_ref, sin_ref, o_ref, state_ref):
    @pl.when(pl.program_id(0) == 0)
    def _():
        state_ref[...] = jnp.zeros_like(state_ref)

    t = C_BLOCK
    cos = cos_ref[...]
    sin = sin_ref[...]
    lane = lax.broadcasted_iota(jnp.int32, (t, LANES), 1)
    first_half = (lane % C_QK_DIM) < (C_QK_DIM // 2)

    def rotary(x):
        swapped = jnp.where(first_half, pltpu.roll(x, LANES - C_QK_DIM // 2, 1),
                            pltpu.roll(x, C_QK_DIM // 2, 1))
        return x * cos + swapped * sin

    ti = lax.broadcasted_iota(jnp.int32, (t, t), 0)
    si = lax.broadcasted_iota(jnp.int32, (t, t), 1)
    rel = (ti - si).astype(F32)
    pos = lax.broadcasted_iota(jnp.int32, (t, C_QK_DIM), 0).astype(F32)

    for pair in range(C_HEADS // 2):
        lanes = slice(pair * LANES, (pair + 1) * LANES)
        q2 = rotary(q_ref[:, lanes])
        k2 = rotary(k_ref[:, lanes]) * (C_QK_DIM ** -0.5)
        for sub in range(2):
            h = 2 * pair + sub
            log_gamma = np.float32(np.log(1.0 - 2.0 ** (-5.0 - h)))
            qh = q2[:, sub * C_QK_DIM:(sub + 1) * C_QK_DIM]
            kh = k2[:, sub * C_QK_DIM:(sub + 1) * C_QK_DIM]
            vh = v_ref[:, h * C_V_DIM:(h + 1) * C_V_DIM]
            decay = jnp.where(rel >= 0.0, jnp.exp(log_gamma * jnp.maximum(rel, 0.0)), 0.0)
            scores = lax.dot_general(qh.astype(BF16), kh.astype(BF16), NT_DIMS,
                                     preferred_element_type=F32) * decay
            o = jnp.dot(scores.astype(BF16), vh, preferred_element_type=F32)
            state = state_ref[h]
            q_dec = qh * jnp.exp(log_gamma * (pos + 1.0))
            o = o + jnp.dot(q_dec.astype(BF16), state.astype(BF16), preferred_element_type=F32)
            k_dec = kh * jnp.exp(log_gamma * (t - 1.0 - pos))
            state_ref[h] = (np.float32(np.exp(log_gamma * t)) * state
                            + lax.dot_general(k_dec.astype(BF16), vh, TN_DIMS, preferred_element_type=F32))
            gate = _silu(g_ref[:, h * C_V_DIM:(h + 1) * C_V_DIM])
            o_ref[:, h * C_V_DIM:(h + 1) * C_V_DIM] = (_rms(o) * gate).astype(o_ref.dtype)


def _rotary_tables(seq):
    half = C_QK_DIM // 2
    inv_freq = jnp.asarray(1.0 / ROPE_BASE ** np.linspace(0.0, 1.0, half), F32)
    ang = jnp.arange(seq).astype(F32)[:, None] * inv_freq[None, :]
    cos, sin = jnp.cos(ang), jnp.sin(ang)
    reps = LANES // C_QK_DIM
    return (jnp.tile(jnp.concatenate([cos, cos], axis=1), (1, reps)),
            jnp.tile(jnp.concatenate([-sin, sin], axis=1), (1, reps)))


def _retention(pf, pb, cos, sin, *, q_col, k_col, g_col, v_col):
    s = pf.shape[0]
    t = C_BLOCK
    return pl.pallas_call(
        _retention_kernel,
        grid=(s // t,),
        in_specs=[
            pl.BlockSpec((t, C_QK_WIDTH), lambda i: (i, q_col)),
            pl.BlockSpec((t, C_QK_WIDTH), lambda i: (i, k_col)),
            pl.BlockSpec((t, C_V_WIDTH), lambda i: (i, v_col)),
            pl.BlockSpec((t, C_V_WIDTH), lambda i: (i, g_col)),
            pl.BlockSpec((t, LANES), lambda i: (i, 0)),
            pl.BlockSpec((t, LANES), lambda i: (i, 0)),
        ],
        out_specs=pl.BlockSpec((t, C_V_WIDTH), lambda i: (i, 0)),
        out_shape=jax.ShapeDtypeStruct((s, C_V_WIDTH), BF16),
        scratch_shapes=[pltpu.VMEM((C_HEADS, C_QK_DIM, C_V_DIM), F32)],
        compiler_params=_params("arbitrary"),
        name="retention",
    )(pf, pf, pb, pf, cos, sin)


def _merge_kernel(ya_ref, yb_ref, yc_ref, ga_ref, gb_ref, gc_ref, x_ref, wa_ref, wb_ref, wc_ref,
                  wo_ref, xo_ref):
    merged = ga_ref[...] * jnp.dot(ya_ref[...], wa_ref[...], preferred_element_type=F32)
    merged = merged + gb_ref[...] * jnp.dot(yb_ref[...], wb_ref[...], preferred_element_type=F32)
    merged = merged + gc_ref[...] * jnp.dot(yc_ref[...], wc_ref[...], preferred_element_type=F32)
    xo_ref[...] = x_ref[...] + jnp.dot(merged.astype(BF16), wo_ref[...], preferred_element_type=F32)


def _merge_out(ya, yb, yc, gates, x, wa, wb, wc, wo, *, tm):
    s, d = x.shape
    once = pl.Buffered(1)
    y_spec = pl.BlockSpec((tm, A_WIDTH), lambda i: (i, 0))
    gate_spec = lambda c: pl.BlockSpec((tm, d), lambda i: (i, c))
    w_spec = pl.BlockSpec((A_WIDTH, d), lambda i: (0, 0), pipeline_mode=once)
    return pl.pallas_call(
        _merge_kernel,
        grid=(s // tm,),
        in_specs=[y_spec, y_spec, y_spec, gate_spec(0), gate_spec(1), gate_spec(2),
                  pl.BlockSpec((tm, d), lambda i: (i, 0)),
                  w_spec, w_spec, w_spec,
                  pl.BlockSpec((d, d), lambda i: (0, 0), pipeline_mode=once)],
        out_specs=pl.BlockSpec((tm, d), lambda i: (i, 0)),
        out_shape=jax.ShapeDtypeStruct((s, d), F32),
        compiler_params=_params("parallel"),
        name="merge_out",
    )(ya, yb, yc, gates, gates, gates, x, wa, wb, wc, wo)


def _mlp_kernel(x_ref, g_ref, wu_ref, wd_ref, o_ref, h_ref):
    @pl.when(pl.program_id(1) == 0)
    def _():
        x = x_ref[...]
        h_ref[...] = _rms(x, g_ref[...]).astype(BF16)
        o_ref[...] = x

    up = jnp.maximum(jnp.dot(h_ref[...], wu_ref[...], preferred_element_type=F32), 0.0)
    o_ref[...] += jnp.dot((up * up).astype(BF16), wd_ref[...], preferred_element_type=F32)


def _mlp(x, g, wu, wd, *, tm, tf):
    s, d = x.shape
    ff = wu.shape[1]
    return pl.pallas_call(
        _mlp_kernel,
        grid=(s // tm, ff // tf),
        in_specs=[pl.BlockSpec((tm, d), lambda i, f: (i, 0), pipeline_mode=pl.Buffered(1)),
                  pl.BlockSpec((1, d), lambda i, f: (0, 0)),
                  pl.BlockSpec((d, tf), lambda i, f: (0, f)),
                  pl.BlockSpec((tf, d), lambda i, f: (f, 0))],
        out_specs=pl.BlockSpec((tm, d), lambda i, f: (i, 0)),
        out_shape=jax.ShapeDtypeStruct((s, d), F32),
        scratch_shapes=[pltpu.VMEM((tm, d), BF16)],
        compiler_params=_params("parallel", "arbitrary"),
        name="mlp",
    )(x, g, wu, wd)


def _split_w_in(w):
    sizes = (A_WIDTH, A_WIDTH, A_WIDTH, B_WIDTH, B_WIDTH, B_WIDTH, B_WIDTH,
             C_QK_WIDTH, C_QK_WIDTH, C_V_WIDTH, C_V_WIDTH)
    aq, ak, av, bf, bq, bi, bg, cq, ck, cv, cg = jnp.split(w, np.cumsum(sizes)[:-1], axis=1)
    return (jnp.concatenate([aq, ak, bf, bq, bg, cq, ck, cg], axis=1),
            jnp.concatenate([av, bi, cv], axis=1))


def _layer(x, p, lower_bound, cos, sin, *, tm_proj, tm_merge, tm_mlp, tf_mlp, tb_hgrn):
    (norm_mix_g, w_in, qn_g, kn_g, rel_bias, hgrn_norm_g, w_gate, b_gate,
     w_br_a, w_br_b, w_br_c, w_out, norm_ffn_g, w_up, w_down) = p
    w_f, w_b = _split_w_in(w_in)
    g = norm_mix_g.reshape(1, D_MODEL)
    zeros = lambda w: jnp.zeros((1, w.shape[1]), F32)
    pf = _norm_matmul(x, g, w_f.astype(BF16), zeros(w_f), act=None, out_dtype=F32,
                      tm=tm_proj, tn=1024, name="proj_f32")
    pb = _norm_matmul(x, g, w_b.astype(BF16), zeros(w_b), act=None, out_dtype=BF16,
                      tm=tm_proj, tn=1024, name="proj_bf16")
    gates = _norm_matmul(x, g, w_gate.astype(BF16), b_gate.reshape(1, -1), act="sigmoid", out_dtype=F32,
                         tm=tm_proj, tn=1024, name="proj_gates")

    ya = _attention(pf, pb, _rel_bias_window(rel_bias), qn_g.reshape(1, -1), kn_g.reshape(1, -1),
                    q_col=0, k_col=A_WIDTH // A_HEAD_DIM, v_col=0)
    yb = _hgrn(pf, pb, lower_bound, hgrn_norm_g.reshape(1, -1),
               z_col=2, q_col=3, g_col=4, v_col=1, tb=tb_hgrn)
    yc = _retention(pf, pb, cos, sin, q_col=5120 // C_QK_WIDTH, k_col=5632 // C_QK_WIDTH,
                    g_col=6144 // C_V_WIDTH, v_col=2)

    x_new = _merge_out(ya, yb, yc, gates, x, w_br_a.astype(BF16), w_br_b.astype(BF16),
                       w_br_c.astype(BF16), w_out.astype(BF16), tm=tm_merge)
    return _mlp(x_new, norm_ffn_g.reshape(1, -1), w_up.astype(BF16), w_down.astype(BF16),
                tm=tm_mlp, tf=tf_mlp)


def _block(x, norm_mix_g, w_in, qn_g, kn_g, rel_bias, lb_logits, hgrn_norm_g, w_gate, b_gate,
           w_br_a, w_br_b, w_br_c, w_out, norm_ffn_g, w_up, w_down, **tiles):
    bsz, seq, d = x.shape
    depth = w_in.shape[0]
    lb_cum = jnp.cumsum(jax.nn.softmax(lb_logits.astype(F32), axis=0), axis=0)
    lower_bounds = lb_cum - lb_cum[0:1]
    cos, sin = _rotary_tables(seq)
    outs = []
    for b in range(bsz):
        xb = x[b]
        for l in range(depth):
            p = (norm_mix_g[l], w_in[l], qn_g[l], kn_g[l], rel_bias[l], hgrn_norm_g[l], w_gate[l],
                 b_gate[l], w_br_a[l], w_br_b[l], w_br_c[l], w_out[l], norm_ffn_g[l], w_up[l], w_down[l])
            xb = _layer(xb, p, lower_bounds[l], cos, sin, **tiles)
        outs.append(xb)
    return jnp.stack(outs, axis=0)


def kernel(x, norm_mix_g, w_in, qn_g, kn_g, rel_bias, lb_logits, hgrn_norm_g, w_gate, b_gate,
           w_br_a, w_br_b, w_br_c, w_out, norm_ffn_g, w_up, w_down):
    return _block(x, norm_mix_g, w_in, qn_g, kn_g, rel_bias, lb_logits, hgrn_norm_g, w_gate, b_gate,
                  w_br_a, w_br_b, w_br_c, w_out, norm_ffn_g, w_up, w_down,
                  tm_proj=1024, tm_merge=256, tm_mlp=1024, tf_mlp=512, tb_hgrn=256)
```

```python
import functools

import numpy as np
import jax
import jax.numpy as jnp
from jax import lax
from jax.experimental import pallas as pl
from jax.experimental.pallas import tpu as pltpu

D_MODEL = 2048
CHUNK = 64
EPS = 1e-6

A_HEADS = 8
A_HEAD_DIM = 128
A_WIDTH = A_HEADS * A_HEAD_DIM
A_PAST_CHUNKS = 8
REL_CLIP = 256

B_HEADS = 8
B_KEY_DIM = 128
B_VAL_DIM = 128
B_WIDTH = B_HEADS * B_KEY_DIM

C_HEADS = 8
C_QK_DIM = 64
C_V_DIM = 128
C_QK_WIDTH = C_HEADS * C_QK_DIM
C_V_WIDTH = C_HEADS * C_V_DIM
ROPE_BASE = 10000.0

N_BRANCH = 3

LANES = 128
VMEM_LIMIT = 56 * 1024 * 1024

BF16 = jnp.bfloat16
F32 = jnp.float32
NT_DIMS = (((1,), (1,)), ((), ()))
TN_DIMS = (((0,), (0,)), ((), ()))
LOG2E = float(np.log2(np.e))


def _params(*semantics):
    return pltpu.CompilerParams(dimension_semantics=semantics, vmem_limit_bytes=VMEM_LIMIT)


def _rms(t, gain=None):
    y = t * lax.rsqrt(jnp.mean(t * t, axis=-1, keepdims=True) + EPS)
    return y if gain is None else y * gain


def _silu(t):
    return t / (1.0 + jnp.exp(-t))


def _neg_abs(t):
    bits = lax.bitcast_convert_type(t, jnp.uint32) | jnp.uint32(0x80000000)
    return lax.bitcast_convert_type(bits, F32)


def _norm_matmul_kernel(*refs, act, has_bias):
    if has_bias:
        x_ref, g_ref, w_ref, b_ref, o_ref, h_ref = refs
    else:
        x_ref, g_ref, w_ref, o_ref, h_ref = refs

    @pl.when(pl.program_id(1) == 0)
    def _():
        h_ref[...] = _rms(x_ref[...], g_ref[...]).astype(BF16)

    acc = jnp.dot(h_ref[...], w_ref[...], preferred_element_type=F32)
    if has_bias:
        acc = acc + b_ref[...]
    if act == "sigmoid":
        acc = jax.nn.sigmoid(acc)
    o_ref[...] = acc.astype(o_ref.dtype)


def _norm_matmul(x, g, w, b, layer, *, n_tiles, tile_of, act, out_dtype, tm, tn, name):
    s, d = x.shape
    in_specs = [
        pl.BlockSpec((tm, d), lambda i, j: (i, 0)),
        pl.BlockSpec((None, 1, d), lambda i, j: (layer, 0, 0)),
        pl.BlockSpec((None, d, tn), lambda i, j: (layer, 0, tile_of(j))),
    ]
    args = [x, g, w]
    if b is not None:
        in_specs.append(pl.BlockSpec((None, 1, tn), lambda i, j: (layer, 0, tile_of(j))))
        args.append(b)
    return pl.pallas_call(
        functools.partial(_norm_matmul_kernel, act=act, has_bias=b is not None),
        grid=(s // tm, n_tiles),
        in_specs=in_specs,
        out_specs=pl.BlockSpec((tm, tn), lambda i, j: (i, j)),
        out_shape=jax.ShapeDtypeStruct((s, n_tiles * tn), out_dtype),
        scratch_shapes=[pltpu.VMEM((tm, d), BF16)],
        compiler_params=_params("parallel", "arbitrary"),
        name=name,
    )(*args)


A_GROUP = A_PAST_CHUNKS * CHUNK
A_SUB = 4 * CHUNK
A_SUB_WINDOW = A_SUB + A_PAST_CHUNKS * CHUNK
A_STEP_GROUPS = 2


def _attn_kernel(q_ref, k_ref, v_ref, bias_ref, gq_ref, gk_ref, o_ref, kbuf, vbuf):
    i = pl.program_id(1)

    def attend(base, first_block):
        q = (_rms(q_ref[base:base + A_GROUP, :], gq_ref[...]) * (A_HEAD_DIM ** -0.5)).astype(BF16)
        for lo in range(0, A_GROUP, A_SUB):
            s = lax.dot_general(q[lo:lo + A_SUB, :], kbuf[lo:lo + A_SUB_WINDOW, :], NT_DIMS,
                                preferred_element_type=F32) + bias_ref[...]
            if first_block:
                col = lax.broadcasted_iota(jnp.int32, s.shape, 1)
                s = jnp.where(col >= A_GROUP - lo, s, -jnp.inf)
            m = jnp.max(s, axis=-1, keepdims=True)
            p = jnp.exp(s - m)
            l = jnp.sum(p, axis=-1, keepdims=True)
            o = jnp.dot(p.astype(BF16), vbuf[lo:lo + A_SUB_WINDOW, :], preferred_element_type=F32)
            o_ref[base + lo:base + lo + A_SUB, :] = (o / l).astype(o_ref.dtype)

    def shift_window():
        kbuf[0:A_GROUP, :] = kbuf[A_GROUP:, :]
        vbuf[0:A_GROUP, :] = vbuf[A_GROUP:, :]

    def clear_window():
        kbuf[0:A_GROUP, :] = jnp.zeros((A_GROUP, A_HEAD_DIM), BF16)
        vbuf[0:A_GROUP, :] = jnp.zeros((A_GROUP, A_HEAD_DIM), BF16)

    for group in range(A_STEP_GROUPS):
        base = group * A_GROUP
        if group == 0:
            pl.when(i == 0)(clear_window)
            pl.when(i > 0)(shift_window)
        else:
            shift_window()
        kbuf[A_GROUP:, :] = _rms(k_ref[base:base + A_GROUP, :], gk_ref[...]).astype(BF16)
        vbuf[A_GROUP:, :] = v_ref[base:base + A_GROUP, :]
        if group == 0:
            pl.when(i == 0)(functools.partial(attend, base, True))
            pl.when(i > 0)(functools.partial(attend, base, False))
        else:
            attend(base, False)


def _attention(pf, pb, bias, gq, gk, layer, *, q_col, k_col, v_col):
    s = pf.shape[0]
    blk = (A_STEP_GROUPS * A_GROUP, A_HEAD_DIM)
    cur = lambda c: (lambda h, i: (i, c + h))
    gain_spec = pl.BlockSpec((None, 1, A_HEAD_DIM), lambda h, i: (layer, 0, 0))
    return pl.pallas_call(
        _attn_kernel,
        grid=(A_HEADS, s // blk[0]),
        in_specs=[
            pl.BlockSpec(blk, cur(q_col)),
            pl.BlockSpec(blk, cur(k_col)),
            pl.BlockSpec(blk, cur(v_col)),
            pl.BlockSpec((None, None, A_SUB, A_SUB_WINDOW), lambda h, i: (layer, h, 0, 0)),
            gain_spec, gain_spec,
        ],
        out_specs=pl.BlockSpec(blk, lambda h, i: (i, h)),
        out_shape=jax.ShapeDtypeStruct((s, A_WIDTH), BF16),
        scratch_shapes=[pltpu.VMEM((2 * A_GROUP, A_HEAD_DIM), BF16),
                        pltpu.VMEM((2 * A_GROUP, A_HEAD_DIM), BF16)],
        compiler_params=_params("arbitrary", "arbitrary"),
        name="attention",
    )(pf, pf, pb, bias, gq, gk)


def _rel_bias_window(rel_table):
    t = rel_table.astype(F32)
    lead = t.shape[:-1]
    past = A_PAST_CHUNKS * CHUNK
    n = A_SUB + A_SUB_WINDOW
    n_far = past + A_SUB - 1 - REL_CLIP
    p = jnp.concatenate([jnp.broadcast_to(t[..., 2 * REL_CLIP:], lead + (n_far,)),
                         t[..., ::-1][..., :n - 1 - n_far],
                         jnp.zeros(lead + (1,), F32)], axis=-1)
    v = jnp.roll(p, -(A_SUB - 1), axis=-1)
    rows = jnp.tile(v, A_SUB)[..., :A_SUB * (n - 1)].reshape(lead + (A_SUB, n - 1))
    bias = rows[..., :A_SUB_WINDOW]
    q_chunk = np.arange(A_SUB)[:, None] // CHUNK
    k_chunk = np.arange(A_SUB_WINDOW)[None, :] // CHUNK
    band = (k_chunk >= q_chunk) & (k_chunk <= q_chunk + A_PAST_CHUNKS)
    return jnp.where(band, bias, -jnp.inf)


B_LEVELS = (32, 16, 8, 4, 2, 1)


def _pair_level_matrix():
    t = np.arange(CHUNK)[:, None]
    s = np.arange(CHUNK)[None, :]
    lvl = np.full((CHUNK, CHUNK), -1, np.int32)
    lvl[t == s] = 0
    for n, h in enumerate(B_LEVELS):
        m = (t // (2 * h) == s // (2 * h)) & (t % (2 * h) >= h) & (s % (2 * h) < h)
        lvl[m] = n + 1
    return lvl


def _hgrn_chunk(z, qin, v, log2lb, log2_1mlb, one_m_lb, level, state_t):
    row = lax.broadcasted_iota(jnp.int32, (CHUNK, B_KEY_DIM), 0)
    z2 = z * LOG2E
    e = jnp.exp2(_neg_abs(z2))
    log2_sig = jnp.minimum(z2, 0.0) - jnp.log2(1.0 + e)
    bb = log2_1mlb + log2_sig
    lf = jnp.maximum(log2lb, bb) + jnp.log2(1.0 + jnp.exp2(_neg_abs(log2lb - bb)))
    kb = (one_m_lb * (jnp.where(z >= 0.0, e, 1.0) / (1.0 + e))).astype(BF16)
    qb = _silu(qin).astype(BF16)

    b = lf
    for sh in (1, 2, 4):
        b = b + jnp.where(row >= sh, pltpu.roll(b, sh, 0), 0.0)
    for sh in (8, 16, 32):
        b = b + jnp.concatenate([jnp.zeros((sh, B_KEY_DIM), F32), b[:CHUNK - sh, :]], axis=0)

    def pairs(qw, kw):
        return lax.dot_general(qw, kw, NT_DIMS, preferred_element_type=F32)

    attn = jnp.where(level == 0, pairs(qb, kb), 0.0)
    for n, h in enumerate(B_LEVELS):
        if h >= 4:
            bound = jnp.concatenate(
                [jnp.broadcast_to(b[blk + h - 1:blk + h, :], (2 * h, B_KEY_DIM))
                 for blk in range(0, CHUNK, 2 * h)], axis=0)
            expo = _neg_abs(b - bound)
        elif h == 2:
            r4 = row % 4
            expo = jnp.where(r4 == 0, pltpu.roll(lf, CHUNK - 1, 0),
                             jnp.where(r4 == 1, 0.0,
                                       jnp.where(r4 == 2, lf, lf + pltpu.roll(lf, 1, 0))))
        else:
            expo = jnp.where(row % 2 == 1, lf, 0.0)
        w = jnp.exp2(expo).astype(BF16)
        attn = jnp.where(level == n + 1, pairs(qb * w, kb * w), attn)

    b_last = b[CHUNK - 1:CHUNK, :]
    o_past = lax.dot_general(qb * jnp.exp2(b).astype(BF16), state_t.astype(BF16), NT_DIMS,
                             preferred_element_type=F32)
    k_dec = kb * jnp.exp2(b_last - b).astype(BF16)
    new_state_t = jnp.exp2(b_last) * state_t + lax.dot_general(v, k_dec, TN_DIMS,
                                                                preferred_element_type=F32)
    return attn.astype(BF16), o_past, new_state_t


def _hgrn_chunk_out(attn, o_past, v, g, gain):
    o = o_past + jnp.dot(attn, v, preferred_element_type=F32)
    return _rms(o, gain) * _silu(g)


def _hgrn_kernel(z_ref, q_ref, v_ref, g_ref, log2lb_ref, log2_1mlb_ref, omlb_ref, gain_ref, level_ref,
                 o_ref, state_ref, *, n_chunks):
    @pl.when(pl.program_id(0) == 0)
    def _():
        state_ref[...] = jnp.zeros_like(state_ref)

    level = level_ref[...]
    gain = gain_ref[...]

    def chunk_body(c, carry):
        rows = pl.ds(pl.multiple_of(c * CHUNK, CHUNK), CHUNK)
        head_cols = [slice(h * B_KEY_DIM, (h + 1) * B_KEY_DIM) for h in range(B_HEADS)]
        partial_out = []
        for h, cols in enumerate(head_cols):
            attn, o_past, st = _hgrn_chunk(z_ref[rows, cols], q_ref[rows, cols], v_ref[rows, cols],
                                           log2lb_ref[:, cols], log2_1mlb_ref[:, cols], omlb_ref[:, cols],
                                           level, state_ref[h])
            state_ref[h] = st
            partial_out.append((attn, o_past))
        for (attn, o_past), cols in zip(partial_out, head_cols):
            y = _hgrn_chunk_out(attn, o_past, v_ref[rows, cols], g_ref[rows, cols], gain)
            o_ref[rows, cols] = y.astype(o_ref.dtype)
        return carry

    lax.fori_loop(0, n_chunks, chunk_body, 0)


def _hgrn(pf, pb, bound_consts, gain, layer, *, z_col, q_col, g_col, v_col, tb):
    s = pf.shape[0]
    level = jnp.asarray(_pair_level_matrix())
    wide = lambda c: pl.BlockSpec((tb, B_WIDTH), lambda i: (i, c))
    row = pl.BlockSpec((None, 1, B_WIDTH), lambda i: (layer, 0, 0))
    return pl.pallas_call(
        functools.partial(_hgrn_kernel, n_chunks=tb // CHUNK),
        grid=(s // tb,),
        in_specs=[wide(z_col), wide(q_col), wide(v_col), wide(g_col), row, row, row,
                  pl.BlockSpec((None, 1, B_VAL_DIM), lambda i: (layer, 0, 0)),
                  pl.BlockSpec((CHUNK, CHUNK), lambda i: (0, 0))],
        out_specs=pl.BlockSpec((tb, B_WIDTH), lambda i: (i, 0)),
        out_shape=jax.ShapeDtypeStruct((s, B_WIDTH), BF16),
        scratch_shapes=[pltpu.VMEM((B_HEADS, B_VAL_DIM, B_KEY_DIM), F32)],
        compiler_params=_params("arbitrary"),
        name="hgrn2",
    )(pf, pf, pb, pf, *bound_consts, gain, level)


def _hgrn_bound_consts(lb_logits):
    lb_cum = jnp.cumsum(jax.nn.softmax(lb_logits.astype(F32), axis=0), axis=0)
    lb = (lb_cum - lb_cum[0:1])[:, None, :]
    return jnp.log2(lb), jnp.log1p(-lb) * LOG2E, 1.0 - lb


C_BLOCK = 256


def _retention_kernel(q_ref, k_ref, v_ref, g_ref, cos_ref, sin_ref, o_ref, state_ref, decay_ref):
    t = C_BLOCK
    log_gamma = [np.float32(np.log(1.0 - 2.0 ** (-5.0 - h))) for h in range(C_HEADS)]

    @pl.when(pl.program_id(0) == 0)
    def _():
        state_ref[...] = jnp.zeros_like(state_ref)
        ti = lax.broadcasted_iota(jnp.int32, (t, t), 0)
        si = lax.broadcasted_iota(jnp.int32, (t, t), 1)
        rel = (ti - si).astype(F32)
        for h in range(C_HEADS):
            decay_ref[h] = jnp.where(rel >= 0.0, jnp.exp(log_gamma[h] * jnp.maximum(rel, 0.0)), 0.0)

    cos = cos_ref[...]
    sin = sin_ref[...]
    lane = lax.broadcasted_iota(jnp.int32, (t, LANES), 1)
    first_half = (lane % C_QK_DIM) < (C_QK_DIM // 2)

    def rotary(x):
        swapped = jnp.where(first_half, pltpu.roll(x, LANES - C_QK_DIM // 2, 1),
                            pltpu.roll(x, C_QK_DIM // 2, 1))
        return x * cos + swapped * sin

    pos = lax.broadcasted_iota(jnp.int32, (t, C_QK_DIM), 0).astype(F32)

    partial_out = []
    for pair in range(C_HEADS // 2):
        lanes = slice(pair * LANES, (pair + 1) * LANES)
        q2 = rotary(q_ref[:, lanes])
        k2 = rotary(k_ref[:, lanes]) * (C_QK_DIM ** -0.5)
        for sub in range(2):
            h = 2 * pair + sub
            qh = q2[:, sub * C_QK_DIM:(sub + 1) * C_QK_DIM]
            kh = k2[:, sub * C_QK_DIM:(sub + 1) * C_QK_DIM]
            vh = v_ref[:, h * C_V_DIM:(h + 1) * C_V_DIM]
            scores = lax.dot_general(qh.astype(BF16), kh.astype(BF16), NT_DIMS,
                                     preferred_element_type=F32) * decay_ref[h]
            state = state_ref[h]
            q_dec = qh * jnp.exp(log_gamma[h] * (pos + 1.0))
            o_past = jnp.dot(q_dec.astype(BF16), state.astype(BF16), preferred_element_type=F32)
            k_dec = kh * jnp.exp(log_gamma[h] * (t - 1.0 - pos))
            state_ref[h] = (np.float32(np.exp(log_gamma[h] * t)) * state
                            + lax.dot_general(k_dec.astype(BF16), vh, TN_DIMS, preferred_element_type=F32))
            partial_out.append((scores.astype(BF16), o_past))
    for h, (scores, o_past) in enumerate(partial_out):
        cols = slice(h * C_V_DIM, (h + 1) * C_V_DIM)
        o = o_past + jnp.dot(scores, v_ref[:, cols], preferred_element_type=F32)
        o_ref[:, cols] = (_rms(o) * _silu(g_ref[:, cols])).astype(o_ref.dtype)


def _rotary_tables(seq):
    half = C_QK_DIM // 2
    inv_freq = jnp.asarray(1.0 / ROPE_BASE ** np.linspace(0.0, 1.0, half), F32)
    ang = jnp.arange(seq).astype(F32)[:, None] * inv_freq[None, :]
    cos, sin = jnp.cos(ang), jnp.sin(ang)
    reps = LANES // C_QK_DIM
    return (jnp.tile(jnp.concatenate([cos, cos], axis=1), (1, reps)),
            jnp.tile(jnp.concatenate([-sin, sin], axis=1), (1, reps)))


def _retention(pf, pb, cos, sin, *, q_col, k_col, g_col, v_col):
    s = pf.shape[0]
    t = C_BLOCK
    return pl.pallas_call(
        _retention_kernel,
        grid=(s // t,),
        in_specs=[
            pl.BlockSpec((t, C_QK_WIDTH), lambda i: (i, q_col)),
            pl.BlockSpec((t, C_QK_WIDTH), lambda i: (i, k_col)),
            pl.BlockSpec((t, C_V_WIDTH), lambda i: (i, v_col)),
            pl.BlockSpec((t, C_V_WIDTH), lambda i: (i, g_col)),
            pl.BlockSpec((t, LANES), lambda i: (i, 0)),
            pl.BlockSpec((t, LANES), lambda i: (i, 0)),
        ],
        out_specs=pl.BlockSpec((t, C_V_WIDTH), lambda i: (i, 0)),
        out_shape=jax.ShapeDtypeStruct((s, C_V_WIDTH), BF16),
        scratch_shapes=[pltpu.VMEM((C_HEADS, C_QK_DIM, C_V_DIM), F32),
                        pltpu.VMEM((C_HEADS, t, t), F32)],
        compiler_params=_params("arbitrary"),
        name="retention",
    )(pf, pf, pb, pf, cos, sin)


def _merge_kernel(ya_ref, yb_ref, yc_ref, ga_ref, gb_ref, gc_ref, x_ref, wa_ref, wb_ref, wc_ref,
                  wo_ref, xo_ref):
    merged = ga_ref[...] * jnp.dot(ya_ref[...], wa_ref[...], preferred_element_type=F32)
    merged = merged + gb_ref[...] * jnp.dot(yb_ref[...], wb_ref[...], preferred_element_type=F32)
    merged = merged + gc_ref[...] * jnp.dot(yc_ref[...], wc_ref[...], preferred_element_type=F32)
    xo_ref[...] = x_ref[...] + jnp.dot(merged.astype(BF16), wo_ref[...], preferred_element_type=F32)


def _merge_out(ya, yb, yc, gates, x, wa, wb, wc, wo, layer, *, tm):
    s, d = x.shape
    once = pl.Buffered(1)
    y_spec = pl.BlockSpec((tm, A_WIDTH), lambda i: (i, 0))
    gate_spec = lambda c: pl.BlockSpec((tm, d), lambda i: (i, c))
    w_spec = pl.BlockSpec((None, A_WIDTH, d), lambda i: (layer, 0, 0), pipeline_mode=once)
    return pl.pallas_call(
        _merge_kernel,
        grid=(s // tm,),
        in_specs=[y_spec, y_spec, y_spec, gate_spec(0), gate_spec(1), gate_spec(2),
                  pl.BlockSpec((tm, d), lambda i: (i, 0)),
                  w_spec, w_spec, w_spec,
                  pl.BlockSpec((None, d, d), lambda i: (layer, 0, 0), pipeline_mode=once)],
        out_specs=pl.BlockSpec((tm, d), lambda i: (i, 0)),
        out_shape=jax.ShapeDtypeStruct((s, d), F32),
        compiler_params=_params("parallel"),
        name="merge_out",
    )(ya, yb, yc, gates, gates, gates, x, wa, wb, wc, wo)


def _mlp_kernel(x_ref, g_ref, wu_ref, wd_ref, o_ref, h_ref):
    @pl.when(pl.program_id(1) == 0)
    def _():
        x = x_ref[...]
        h_ref[...] = _rms(x, g_ref[...]).astype(BF16)
        o_ref[...] = x

    up = jnp.maximum(jnp.dot(h_ref[...], wu_ref[...], preferred_element_type=F32), 0.0)
    o_ref[...] += jnp.dot((up * up).astype(BF16), wd_ref[...], preferred_element_type=F32)


def _mlp(x, g, wu, wd, layer, *, tm, tf):
    s, d = x.shape
    ff = wu.shape[-1]
    return pl.pallas_call(
        _mlp_kernel,
        grid=(s // tm, ff // tf),
        in_specs=[pl.BlockSpec((tm, d), lambda i, f: (i, 0), pipeline_mode=pl.Buffered(1)),
                  pl.BlockSpec((None, 1, d), lambda i, f: (layer, 0, 0)),
                  pl.BlockSpec((None, d, tf), lambda i, f: (layer, 0, f)),
                  pl.BlockSpec((None, tf, d), lambda i, f: (layer, f, 0))],
        out_specs=pl.BlockSpec((tm, d), lambda i, f: (i, 0)),
        out_shape=jax.ShapeDtypeStruct((s, d), F32),
        scratch_shapes=[pltpu.VMEM((tm, d), BF16)],
        compiler_params=_params("parallel", "arbitrary"),
        name="mlp",
    )(x, g, wu, wd)


PROJ_TILE = 1024
F32_TILES = (0, 1, 3, 4, 6, 7, 9)
BF16_TILES = (2, 5, 8)


def _f32_tile_of(j):
    return j + (j >= 2).astype(jnp.int32) + (j >= 4).astype(jnp.int32) + (j >= 6).astype(jnp.int32)


def _bf16_tile_of(j):
    return 3 * j + 2


def _layer(x, layer, w, consts, *, tm_proj, tm_merge, tm_mlp, tf_mlp, tb_hgrn):
    pf = _norm_matmul(x, w["norm_mix_g"], w["w_in"], None, layer, n_tiles=len(F32_TILES), tile_of=_f32_tile_of,
                      act=None, out_dtype=F32, tm=tm_proj, tn=PROJ_TILE, name="proj_f32")
    pb = _norm_matmul(x, w["norm_mix_g"], w["w_in"], None, layer, n_tiles=len(BF16_TILES),
                      tile_of=_bf16_tile_of, act=None, out_dtype=BF16, tm=tm_proj, tn=PROJ_TILE, name="proj_bf16")
    gates = _norm_matmul(x, w["norm_mix_g"], w["w_gate"], w["b_gate"], layer,
                         n_tiles=N_BRANCH * D_MODEL // PROJ_TILE, tile_of=lambda j: j, act="sigmoid",
                         out_dtype=F32, tm=tm_proj, tn=PROJ_TILE, name="proj_gates")

    ya = _attention(pf, pb, consts["rel_bias"], w["qn_g"], w["kn_g"], layer,
                    q_col=0, k_col=A_WIDTH // A_HEAD_DIM, v_col=0)
    yb = _hgrn(pf, pb, consts["hgrn_bounds"], w["hgrn_norm_g"], layer,
               z_col=2, q_col=3, g_col=4, v_col=1, tb=tb_hgrn)
    yc = _retention(pf, pb, consts["cos"], consts["sin"], q_col=5120 // C_QK_WIDTH, k_col=5632 // C_QK_WIDTH,
                    g_col=6144 // C_V_WIDTH, v_col=2)

    x_new = _merge_out(ya, yb, yc, gates, x, w["w_br_a"], w["w_br_b"], w["w_br_c"], w["w_out"], layer,
                       tm=tm_merge)
    return _mlp(x_new, w["norm_ffn_g"], w["w_up"], w["w_down"], layer, tm=tm_mlp, tf=tf_mlp)


def _block(x, norm_mix_g, w_in, qn_g, kn_g, rel_bias, lb_logits, hgrn_norm_g, w_gate, b_gate,
           w_br_a, w_br_b, w_br_c, w_out, norm_ffn_g, w_up, w_down, **tiles):
    bsz, seq, d = x.shape
    depth = w_in.shape[0]
    row = lambda t: t.astype(F32)[:, None, :]
    w = dict(norm_mix_g=row(norm_mix_g), w_in=w_in.astype(BF16), qn_g=row(qn_g), kn_g=row(kn_g),
             hgrn_norm_g=row(hgrn_norm_g), w_gate=w_gate.astype(BF16), b_gate=row(b_gate),
             w_br_a=w_br_a.astype(BF16), w_br_b=w_br_b.astype(BF16), w_br_c=w_br_c.astype(BF16),
             w_out=w_out.astype(BF16), norm_ffn_g=row(norm_ffn_g), w_up=w_up.astype(BF16),
             w_down=w_down.astype(BF16))
    cos, sin = _rotary_tables(seq)
    consts = dict(rel_bias=_rel_bias_window(rel_bias), hgrn_bounds=_hgrn_bound_consts(lb_logits),
                  cos=cos, sin=sin)
    rows = x.reshape(bsz * seq, d)
    outs = []
    for b in range(bsz):
        xb = rows[b * seq:(b + 1) * seq]
        for layer in range(depth):
            xb = _layer(xb, layer, w, consts, **tiles)
        outs.append(xb)
    return jnp.concatenate(outs, axis=0).reshape(bsz, seq, d)


def kernel(x, norm_mix_g, w_in, qn_g, kn_g, rel_bias, lb_logits, hgrn_norm_g, w_gate, b_gate,
           w_br_a, w_br_b, w_br_c, w_out, norm_ffn_g, w_up, w_down):
    return _block(x, norm_mix_g, w_in, qn_g, kn_g, rel_bias, lb_logits, hgrn_norm_g, w_gate, b_gate,
                  w_br_a, w_br_b, w_br_c, w_out, norm_ffn_g, w_up, w_down,
                  tm_proj=1024, tm_merge=256, tm_mlp=1024, tf_mlp=1024, tb_hgrn=256)
```

```python
import functools

import numpy as np
import jax
import jax.numpy as jnp
from jax import lax
from jax.experimental import pallas as pl
from jax.experimental.pallas import tpu as pltpu

D_MODEL = 2048
CHUNK = 64
EPS = 1e-6

A_HEADS = 8
A_HEAD_DIM = 128
A_WIDTH = A_HEADS * A_HEAD_DIM
A_PAST_CHUNKS = 8
REL_CLIP = 256

B_HEADS = 8
B_KEY_DIM = 128
B_VAL_DIM = 128
B_WIDTH = B_HEADS * B_KEY_DIM

C_HEADS = 8
C_QK_DIM = 64
C_V_DIM = 128
C_QK_WIDTH = C_HEADS * C_QK_DIM
C_V_WIDTH = C_HEADS * C_V_DIM
ROPE_BASE = 10000.0

N_BRANCH = 3

LANES = 128
VMEM_LIMIT = 56 * 1024 * 1024

BF16 = jnp.bfloat16
F32 = jnp.float32
NT_DIMS = (((1,), (1,)), ((), ()))
TN_DIMS = (((0,), (0,)), ((), ()))
LOG2E = float(np.log2(np.e))


def _params(*semantics):
    return pltpu.CompilerParams(dimension_semantics=semantics, vmem_limit_bytes=VMEM_LIMIT)


def _rms(t, gain=None):
    y = t * lax.rsqrt(jnp.mean(t * t, axis=-1, keepdims=True) + EPS)
    return y if gain is None else y * gain


def _silu(t):
    return t / (1.0 + jnp.exp(-t))


def _neg_abs(t):
    bits = lax.bitcast_convert_type(t, jnp.uint32) | jnp.uint32(0x80000000)
    return lax.bitcast_convert_type(bits, F32)


def _norm_matmul_kernel(x_ref, g_ref, w_ref, o_ref, h_ref):
    @pl.when(pl.program_id(1) == 0)
    def _():
        h_ref[...] = _rms(x_ref[...], g_ref[...]).astype(BF16)

    o_ref[...] = jnp.dot(h_ref[...], w_ref[...], preferred_element_type=F32).astype(o_ref.dtype)


def _norm_matmul(x, g, w, layer, *, n_tiles, tile_of, out_dtype, tm, tn, name):
    s, d = x.shape
    return pl.pallas_call(
        _norm_matmul_kernel,
        grid=(s // tm, n_tiles),
        in_specs=[
            pl.BlockSpec((tm, d), lambda i, j: (i, 0)),
            pl.BlockSpec((None, 1, d), lambda i, j: (layer, 0, 0)),
            pl.BlockSpec((None, d, tn), lambda i, j: (layer, 0, tile_of(j))),
        ],
        out_specs=pl.BlockSpec((tm, tn), lambda i, j: (i, j)),
        out_shape=jax.ShapeDtypeStruct((s, n_tiles * tn), out_dtype),
        scratch_shapes=[pltpu.VMEM((tm, d), BF16)],
        compiler_params=_params("parallel", "arbitrary"),
        name=name,
    )(x, g, w)


A_GROUP = A_PAST_CHUNKS * CHUNK
A_SUB = 4 * CHUNK
A_SUB_WINDOW = A_SUB + A_PAST_CHUNKS * CHUNK
A_STEP = 2 * A_GROUP


def _attn_kernel(q_ref, k_ref, v_ref, bias_ref, gq_ref, gk_ref, o_ref, kbuf, vbuf):
    i = pl.program_id(1)

    @pl.when(i == 0)
    def _():
        kbuf[0:A_GROUP, :] = jnp.zeros((A_GROUP, A_HEAD_DIM), BF16)
        vbuf[0:A_GROUP, :] = jnp.zeros((A_GROUP, A_HEAD_DIM), BF16)

    @pl.when(i > 0)
    def _():
        kbuf[0:A_GROUP, :] = kbuf[A_STEP:, :]
        vbuf[0:A_GROUP, :] = vbuf[A_STEP:, :]

    kbuf[A_GROUP:, :] = _rms(k_ref[...], gk_ref[...]).astype(BF16)
    vbuf[A_GROUP:, :] = v_ref[...]
    q = (_rms(q_ref[...], gq_ref[...]) * (A_HEAD_DIM ** -0.5 * LOG2E)).astype(BF16)

    def attend(first_step):
        subs = range(0, A_STEP, A_SUB)
        scores = []
        for lo in subs:
            s = lax.dot_general(q[lo:lo + A_SUB, :], kbuf[lo:lo + A_SUB_WINDOW, :], NT_DIMS,
                                preferred_element_type=F32) + bias_ref[...]
            if first_step and lo < A_GROUP:
                col = lax.broadcasted_iota(jnp.int32, s.shape, 1)
                s = jnp.where(col >= A_GROUP - lo, s, -jnp.inf)
            scores.append(s)
        probs = []
        for s in scores:
            p = jnp.exp2(s - jnp.max(s, axis=-1, keepdims=True))
            probs.append((p.astype(BF16), jnp.sum(p, axis=-1, keepdims=True)))
        for lo, (p, l) in zip(subs, probs):
            o = jnp.dot(p, vbuf[lo:lo + A_SUB_WINDOW, :], preferred_element_type=F32)
            o_ref[lo:lo + A_SUB, :] = (o / l).astype(o_ref.dtype)

    pl.when(i == 0)(functools.partial(attend, True))
    pl.when(i > 0)(functools.partial(attend, False))


def _attention(pf, pb, bias, gq, gk, layer, *, q_col, k_col, v_col):
    s = pf.shape[0]
    blk = (A_STEP, A_HEAD_DIM)
    cur = lambda c: (lambda h, i: (i, c + h))
    gain_spec = pl.BlockSpec((None, 1, A_HEAD_DIM), lambda h, i: (layer, 0, 0))
    return pl.pallas_call(
        _attn_kernel,
        grid=(A_HEADS, s // blk[0]),
        in_specs=[
            pl.BlockSpec(blk, cur(q_col)),
            pl.BlockSpec(blk, cur(k_col)),
            pl.BlockSpec(blk, cur(v_col)),
            pl.BlockSpec((None, None, A_SUB, A_SUB_WINDOW), lambda h, i: (layer, h, 0, 0)),
            gain_spec, gain_spec,
        ],
        out_specs=pl.BlockSpec(blk, lambda h, i: (i, h)),
        out_shape=jax.ShapeDtypeStruct((s, A_WIDTH), BF16),
        scratch_shapes=[pltpu.VMEM((A_GROUP + A_STEP, A_HEAD_DIM), BF16),
                        pltpu.VMEM((A_GROUP + A_STEP, A_HEAD_DIM), BF16)],
        compiler_params=_params("arbitrary", "arbitrary"),
        name="attention",
    )(pf, pf, pb, bias, gq, gk)


def _rel_bias_window(rel_table):
    t = rel_table.astype(F32)
    lead = t.shape[:-1]
    past = A_PAST_CHUNKS * CHUNK
    n = A_SUB + A_SUB_WINDOW
    n_far = past + A_SUB - 1 - REL_CLIP
    p = jnp.concatenate([jnp.broadcast_to(t[..., 2 * REL_CLIP:], lead + (n_far,)),
                         t[..., ::-1][..., :n - 1 - n_far],
                         jnp.zeros(lead + (1,), F32)], axis=-1)
    v = jnp.roll(p, -(A_SUB - 1), axis=-1)
    rows = jnp.tile(v, A_SUB)[..., :A_SUB * (n - 1)].reshape(lead + (A_SUB, n - 1))
    bias = rows[..., :A_SUB_WINDOW]
    q_chunk = np.arange(A_SUB)[:, None] // CHUNK
    k_chunk = np.arange(A_SUB_WINDOW)[None, :] // CHUNK
    band = (k_chunk >= q_chunk) & (k_chunk <= q_chunk + A_PAST_CHUNKS)
    return jnp.where(band, bias * LOG2E, -jnp.inf)


B_LEVELS = (32, 16, 8, 4, 2, 1)


def _pair_level_matrix():
    t = np.arange(CHUNK)[:, None]
    s = np.arange(CHUNK)[None, :]
    lvl = np.full((CHUNK, CHUNK), -1, np.int32)
    lvl[t == s] = 0
    for n, h in enumerate(B_LEVELS):
        m = (t // (2 * h) == s // (2 * h)) & (t % (2 * h) >= h) & (s % (2 * h) < h)
        lvl[m] = n + 1
    return lvl


def _hgrn_chunk_prep(z, qin, log2lb, log2_1mlb, one_m_lb):
    row = lax.broadcasted_iota(jnp.int32, (CHUNK, B_KEY_DIM), 0)
    z2 = z * LOG2E
    e = jnp.exp2(_neg_abs(z2))
    log2_sig = jnp.minimum(z2, 0.0) - jnp.log2(1.0 + e)
    bb = log2_1mlb + log2_sig
    lf = jnp.maximum(log2lb, bb) + jnp.log2(1.0 + jnp.exp2(_neg_abs(log2lb - bb)))
    kb = (one_m_lb * (jnp.where(z >= 0.0, e, 1.0) / (1.0 + e))).astype(BF16)
    qb = _silu(qin).astype(BF16)

    b = lf
    for sh in (1, 2, 4):
        b = b + jnp.where(row >= sh, pltpu.roll(b, sh, 0), 0.0)
    for sh in (8, 16, 32):
        b = b + jnp.concatenate([jnp.zeros((sh, B_KEY_DIM), F32), b[:CHUNK - sh, :]], axis=0)

    operands = [(qb, kb)]
    for h in B_LEVELS:
        if h >= 4:
            bound = jnp.concatenate(
                [jnp.broadcast_to(b[blk + h - 1:blk + h, :], (2 * h, B_KEY_DIM))
                 for blk in range(0, CHUNK, 2 * h)], axis=0)
            expo = _neg_abs(b - bound)
        elif h == 2:
            r4 = row % 4
            expo = jnp.where(r4 == 0, pltpu.roll(lf, CHUNK - 1, 0),
                             jnp.where(r4 == 1, 0.0,
                                       jnp.where(r4 == 2, lf, lf + pltpu.roll(lf, 1, 0))))
        else:
            expo = jnp.where(row % 2 == 1, lf, 0.0)
        w = jnp.exp2(expo).astype(BF16)
        operands.append((qb * w, kb * w))

    b_last = b[CHUNK - 1:CHUNK, :]
    q_dec = qb * jnp.exp2(b).astype(BF16)
    k_dec = kb * jnp.exp2(b_last - b).astype(BF16)
    return operands, q_dec, k_dec, jnp.exp2(b_last)


def _hgrn_chunk_products(prep, v, level, state_t):
    operands, q_dec, k_dec, chunk_decay = prep
    attn = 0.0
    for n, (qw, kw) in enumerate(operands):
        attn = jnp.where(level == n, lax.dot_general(qw, kw, NT_DIMS, preferred_element_type=F32), attn)
    o_past = lax.dot_general(q_dec, state_t.astype(BF16), NT_DIMS, preferred_element_type=F32)
    new_state_t = chunk_decay * state_t + lax.dot_general(v, k_dec, TN_DIMS, preferred_element_type=F32)
    return attn.astype(BF16), o_past, new_state_t


def _hgrn_chunk_out(attn, o_past, v, g, gain):
    o = o_past + jnp.dot(attn, v, preferred_element_type=F32)
    return _rms(o, gain) * _silu(g)


GATE_PIECE = 256


def _gates_hgrn_kernel(x_ref, gm_ref, wg_ref, bg_ref, z_ref, q_ref, v_ref, g_ref, log2lb_ref, log2_1mlb_ref,
                       omlb_ref, gain_ref, level_ref, gates_ref, yb_ref, h_ref, state_ref):
    first_col = pl.program_id(1) == 0

    @pl.when(jnp.logical_and(pl.program_id(0) == 0, first_col))
    def _():
        state_ref[...] = jnp.zeros_like(state_ref)

    @pl.when(first_col)
    def _():
        h_ref[...] = _rms(x_ref[...], gm_ref[...]).astype(BF16)

    level = level_ref[...]
    gain = gain_ref[...]
    head_cols = [slice(h * B_KEY_DIM, (h + 1) * B_KEY_DIM) for h in range(B_HEADS)]

    def gate_piece(p):
        cols = slice(p * GATE_PIECE, (p + 1) * GATE_PIECE)
        acc = jnp.dot(h_ref[...], wg_ref[:, cols], preferred_element_type=F32) + bg_ref[:, cols]
        gates_ref[:, cols] = jax.nn.sigmoid(acc)

    staged = {}

    def prep(c):
        rows = slice(c * CHUNK, (c + 1) * CHUNK)
        staged[c] = [_hgrn_chunk_prep(z_ref[rows, cols], q_ref[rows, cols], log2lb_ref[:, cols],
                                      log2_1mlb_ref[:, cols], omlb_ref[:, cols]) for cols in head_cols]

    def products(c):
        rows = slice(c * CHUNK, (c + 1) * CHUNK)
        for h, cols in enumerate(head_cols):
            attn, o_past, st = _hgrn_chunk_products(staged[c][h], v_ref[rows, cols], level, state_ref[h])
            state_ref[h] = st
            staged[c][h] = (attn, o_past)

    def out(c):
        rows = slice(c * CHUNK, (c + 1) * CHUNK)
        for (attn, o_past), cols in zip(staged[c], head_cols):
            y = _hgrn_chunk_out(attn, o_past, v_ref[rows, cols], g_ref[rows, cols], gain)
            yb_ref[rows, cols] = y.astype(yb_ref.dtype)

    n_chunks = z_ref.shape[0] // CHUNK
    n_pieces = gates_ref.shape[1] // GATE_PIECE
    assert n_pieces > n_chunks
    pieces = iter(range(n_pieces))
    prep(0)
    gate_piece(next(pieces))
    for c in range(n_chunks):
        products(c)
        if c + 1 < n_chunks:
            prep(c + 1)
        gate_piece(next(pieces))
        out(c)
    for p in pieces:
        gate_piece(p)


def _gates_hgrn(x, gm, wg, bg, pf, pb, bound_consts, gain, layer, *, z_col, q_col, g_col, v_col, tm, tn):
    s, d = x.shape
    n_gate = wg.shape[-1]
    n_col = n_gate // tn
    tb = tm // n_col
    assert tb % CHUNK == 0 and tb * n_col == tm and tn % GATE_PIECE == 0
    level = jnp.asarray(_pair_level_matrix())
    wide = lambda c: pl.BlockSpec((tb, B_WIDTH), lambda i, j: (i * n_col + j, c))
    row = pl.BlockSpec((None, 1, B_WIDTH), lambda i, j: (layer, 0, 0))
    return pl.pallas_call(
        _gates_hgrn_kernel,
        grid=(s // tm, n_col),
        in_specs=[pl.BlockSpec((tm, d), lambda i, j: (i, 0)),
                  pl.BlockSpec((None, 1, d), lambda i, j: (layer, 0, 0)),
                  pl.BlockSpec((None, d, tn), lambda i, j: (layer, 0, j)),
                  pl.BlockSpec((None, 1, tn), lambda i, j: (layer, 0, j)),
                  wide(z_col), wide(q_col), wide(v_col), wide(g_col), row, row, row,
                  pl.BlockSpec((None, 1, B_VAL_DIM), lambda i, j: (layer, 0, 0)),
                  pl.BlockSpec((CHUNK, CHUNK), lambda i, j: (0, 0))],
        out_specs=[pl.BlockSpec((tm, tn), lambda i, j: (i, j)),
                   pl.BlockSpec((tb, B_WIDTH), lambda i, j: (i * n_col + j, 0))],
        out_shape=[jax.ShapeDtypeStruct((s, n_gate), F32), jax.ShapeDtypeStruct((s, B_WIDTH), BF16)],
        scratch_shapes=[pltpu.VMEM((tm, d), BF16), pltpu.VMEM((B_HEADS, B_VAL_DIM, B_KEY_DIM), F32)],
        compiler_params=_params("arbitrary", "arbitrary"),
        name="gates_hgrn2",
    )(x, gm, wg, bg, pf, pf, pb, pf, *bound_consts, gain, level)


def _hgrn_bound_consts(lb_logits):
    lb_cum = jnp.cumsum(jax.nn.softmax(lb_logits.astype(F32), axis=0), axis=0)
    lb = (lb_cum - lb_cum[0:1])[:, None, :]
    return jnp.log2(lb), jnp.log1p(-lb) * LOG2E, 1.0 - lb


C_BLOCK = 256


def _retention_kernel(q_ref, k_ref, v_ref, g_ref, cos_ref, sin_ref, o_ref, state_ref, decay_ref):
    t = C_BLOCK
    log_gamma = [np.float32(np.log(1.0 - 2.0 ** (-5.0 - h))) for h in range(C_HEADS)]

    @pl.when(pl.program_id(0) == 0)
    def _():
        state_ref[...] = jnp.zeros_like(state_ref)
        ti = lax.broadcasted_iota(jnp.int32, (t, t), 0)
        si = lax.broadcasted_iota(jnp.int32, (t, t), 1)
        rel = (ti - si).astype(F32)
        for h in range(C_HEADS):
            decay_ref[h] = jnp.where(rel >= 0.0, jnp.exp(log_gamma[h] * jnp.maximum(rel, 0.0)), 0.0)

    cos = cos_ref[...]
    sin = sin_ref[...]
    lane = lax.broadcasted_iota(jnp.int32, (t, LANES), 1)
    first_half = (lane % C_QK_DIM) < (C_QK_DIM // 2)

    def rotary(x):
        swapped = jnp.where(first_half, pltpu.roll(x, LANES - C_QK_DIM // 2, 1),
                            pltpu.roll(x, C_QK_DIM // 2, 1))
        return x * cos + swapped * sin

    pos = lax.broadcasted_iota(jnp.int32, (t, C_QK_DIM), 0).astype(F32)

    partial_out = []
    for pair in range(C_HEADS // 2):
        lanes = slice(pair * LANES, (pair + 1) * LANES)
        q2 = rotary(q_ref[:, lanes])
        k2 = rotary(k_ref[:, lanes]) * (C_QK_DIM ** -0.5)
        for sub in range(2):
            h = 2 * pair + sub
            qh = q2[:, sub * C_QK_DIM:(sub + 1) * C_QK_DIM]
            kh = k2[:, sub * C_QK_DIM:(sub + 1) * C_QK_DIM]
            vh = v_ref[:, h * C_V_DIM:(h + 1) * C_V_DIM]
            scores = lax.dot_general(qh.astype(BF16), kh.astype(BF16), NT_DIMS,
                                     preferred_element_type=F32) * decay_ref[h]
            state = state_ref[h]
            q_dec = qh * jnp.exp(log_gamma[h] * (pos + 1.0))
            o_past = jnp.dot(q_dec.astype(BF16), state.astype(BF16), preferred_element_type=F32)
            k_dec = kh * jnp.exp(log_gamma[h] * (t - 1.0 - pos))
            state_ref[h] = (np.float32(np.exp(log_gamma[h] * t)) * state
                            + lax.dot_general(k_dec.astype(BF16), vh, TN_DIMS, preferred_element_type=F32))
            partial_out.append((scores.astype(BF16), o_past))
    for h, (scores, o_past) in enumerate(partial_out):
        cols = slice(h * C_V_DIM, (h + 1) * C_V_DIM)
        o = o_past + jnp.dot(scores, v_ref[:, cols], preferred_element_type=F32)
        o_ref[:, cols] = (_rms(o) * _silu(g_ref[:, cols])).astype(o_ref.dtype)


def _rotary_tables(seq):
    half = C_QK_DIM // 2
    inv_freq = jnp.asarray(1.0 / ROPE_BASE ** np.linspace(0.0, 1.0, half), F32)
    ang = jnp.arange(seq).astype(F32)[:, None] * inv_freq[None, :]
    cos, sin = jnp.cos(ang), jnp.sin(ang)
    reps = LANES // C_QK_DIM
    return (jnp.tile(jnp.concatenate([cos, cos], axis=1), (1, reps)),
            jnp.tile(jnp.concatenate([-sin, sin], axis=1), (1, reps)))


def _retention(pf, pb, cos, sin, *, q_col, k_col, g_col, v_col):
    s = pf.shape[0]
    t = C_BLOCK
    return pl.pallas_call(
        _retention_kernel,
        grid=(s // t,),
        in_specs=[
            pl.BlockSpec((t, C_QK_WIDTH), lambda i: (i, q_col)),
            pl.BlockSpec((t, C_QK_WIDTH), lambda i: (i, k_col)),
            pl.BlockSpec((t, C_V_WIDTH), lambda i: (i, v_col)),
            pl.BlockSpec((t, C_V_WIDTH), lambda i: (i, g_col)),
            pl.BlockSpec((t, LANES), lambda i: (i, 0)),
            pl.BlockSpec((t, LANES), lambda i: (i, 0)),
        ],
        out_specs=pl.BlockSpec((t, C_V_WIDTH), lambda i: (i, 0)),
        out_shape=jax.ShapeDtypeStruct((s, C_V_WIDTH), BF16),
        scratch_shapes=[pltpu.VMEM((C_HEADS, C_QK_DIM, C_V_DIM), F32),
                        pltpu.VMEM((C_HEADS, t, t), F32)],
        compiler_params=_params("arbitrary"),
        name="retention",
    )(pf, pf, pb, pf, cos, sin)


def _merge_kernel(ya_ref, yb_ref, yc_ref, ga_ref, gb_ref, gc_ref, x_ref, wa_ref, wb_ref, wc_ref,
                  wo_ref, xo_ref):
    merged = ga_ref[...] * jnp.dot(ya_ref[...], wa_ref[...], preferred_element_type=F32)
    merged = merged + gb_ref[...] * jnp.dot(yb_ref[...], wb_ref[...], preferred_element_type=F32)
    merged = merged + gc_ref[...] * jnp.dot(yc_ref[...], wc_ref[...], preferred_element_type=F32)
    xo_ref[...] = x_ref[...] + jnp.dot(merged.astype(BF16), wo_ref[...], preferred_element_type=F32)


def _merge_out(ya, yb, yc, gates, x, wa, wb, wc, wo, layer, *, tm):
    s, d = x.shape
    once = pl.Buffered(1)
    y_spec = pl.BlockSpec((tm, A_WIDTH), lambda i: (i, 0))
    gate_spec = lambda c: pl.BlockSpec((tm, d), lambda i: (i, c))
    w_spec = pl.BlockSpec((None, A_WIDTH, d), lambda i: (layer, 0, 0), pipeline_mode=once)
    return pl.pallas_call(
        _merge_kernel,
        grid=(s // tm,),
        in_specs=[y_spec, y_spec, y_spec, gate_spec(0), gate_spec(1), gate_spec(2),
                  pl.BlockSpec((tm, d), lambda i: (i, 0)),
                  w_spec, w_spec, w_spec,
                  pl.BlockSpec((None, d, d), lambda i: (layer, 0, 0), pipeline_mode=once)],
        out_specs=pl.BlockSpec((tm, d), lambda i: (i, 0)),
        out_shape=jax.ShapeDtypeStruct((s, d), F32),
        compiler_params=_params("parallel"),
        name="merge_out",
    )(ya, yb, yc, gates, gates, gates, x, wa, wb, wc, wo)


def _mlp_kernel(x_ref, g_ref, wu_ref, wd_ref, o_ref, h_ref):
    @pl.when(pl.program_id(1) == 0)
    def _():
        x = x_ref[...]
        h_ref[...] = _rms(x, g_ref[...]).astype(BF16)
        o_ref[...] = x

    up = jnp.maximum(jnp.dot(h_ref[...], wu_ref[...], preferred_element_type=F32), 0.0)
    o_ref[...] += jnp.dot((up * up).astype(BF16), wd_ref[...], preferred_element_type=F32)


def _mlp(x, g, wu, wd, layer, *, tm, tf):
    s, d = x.shape
    ff = wu.shape[-1]
    return pl.pallas_call(
        _mlp_kernel,
        grid=(s // tm, ff // tf),
        in_specs=[pl.BlockSpec((tm, d), lambda i, f: (i, 0), pipeline_mode=pl.Buffered(1)),
                  pl.BlockSpec((None, 1, d), lambda i, f: (layer, 0, 0)),
                  pl.BlockSpec((None, d, tf), lambda i, f: (layer, 0, f)),
                  pl.BlockSpec((None, tf, d), lambda i, f: (layer, f, 0))],
        out_specs=pl.BlockSpec((tm, d), lambda i, f: (i, 0)),
        out_shape=jax.ShapeDtypeStruct((s, d), F32),
        scratch_shapes=[pltpu.VMEM((tm, d), BF16)],
        compiler_params=_params("parallel", "arbitrary"),
        name="mlp",
    )(x, g, wu, wd)


PROJ_TILE = 1024
F32_TILES = (0, 1, 3, 4, 6, 7, 9)
BF16_TILES = (2, 5, 8)


def _f32_tile_of(j):
    return j + (j >= 2).astype(jnp.int32) + (j >= 4).astype(jnp.int32) + (j >= 6).astype(jnp.int32)


def _bf16_tile_of(j):
    return 3 * j + 2


def _layer(x, layer, w, consts, *, tm_proj, tn_gates, tm_merge, tm_mlp, tf_mlp):
    pf = _norm_matmul(x, w["norm_mix_g"], w["w_in"], layer, n_tiles=len(F32_TILES), tile_of=_f32_tile_of,
                      out_dtype=F32, tm=tm_proj, tn=PROJ_TILE, name="proj_f32")
    pb = _norm_matmul(x, w["norm_mix_g"], w["w_in"], layer, n_tiles=len(BF16_TILES), tile_of=_bf16_tile_of,
                      out_dtype=BF16, tm=tm_proj, tn=PROJ_TILE, name="proj_bf16")
    gates, yb = _gates_hgrn(x, w["norm_mix_g"], w["w_gate"], w["b_gate"], pf, pb, consts["hgrn_bounds"],
                            w["hgrn_norm_g"], layer, z_col=2, q_col=3, g_col=4, v_col=1,
                            tm=tm_proj, tn=tn_gates)

    ya = _attention(pf, pb, consts["rel_bias"], w["qn_g"], w["kn_g"], layer,
                    q_col=0, k_col=A_WIDTH // A_HEAD_DIM, v_col=0)
    yc = _retention(pf, pb, consts["cos"], consts["sin"], q_col=5120 // C_QK_WIDTH, k_col=5632 // C_QK_WIDTH,
                    g_col=6144 // C_V_WIDTH, v_col=2)

    x_new = _merge_out(ya, yb, yc, gates, x, w["w_br_a"], w["w_br_b"], w["w_br_c"], w["w_out"], layer,
                       tm=tm_merge)
    return _mlp(x_new, w["norm_ffn_g"], w["w_up"], w["w_down"], layer, tm=tm_mlp, tf=tf_mlp)


def _block(x, norm_mix_g, w_in, qn_g, kn_g, rel_bias, lb_logits, hgrn_norm_g, w_gate, b_gate,
           w_br_a, w_br_b, w_br_c, w_out, norm_ffn_g, w_up, w_down, **tiles):
    bsz, seq, d = x.shape
    depth = w_in.shape[0]
    row = lambda t: t.astype(F32)[:, None, :]
    w = dict(norm_mix_g=row(norm_mix_g), w_in=w_in.astype(BF16), qn_g=row(qn_g), kn_g=row(kn_g),
             hgrn_norm_g=row(hgrn_norm_g), w_gate=w_gate.astype(BF16), b_gate=row(b_gate),
             w_br_a=w_br_a.astype(BF16), w_br_b=w_br_b.astype(BF16), w_br_c=w_br_c.astype(BF16),
             w_out=w_out.astype(BF16), norm_ffn_g=row(norm_ffn_g), w_up=w_up.astype(BF16),
             w_down=w_down.astype(BF16))
    cos, sin = _rotary_tables(seq)
    consts = dict(rel_bias=_rel_bias_window(rel_bias), hgrn_bounds=_hgrn_bound_consts(lb_logits),
                  cos=cos, sin=sin)
    rows = x.reshape(bsz * seq, d)
    outs = []
    for b in range(bsz):
        xb = rows[b * seq:(b + 1) * seq]
        for layer in range(depth):
            xb = _layer(xb, layer, w, consts, **tiles)
        outs.append(xb)
    return jnp.concatenate(outs, axis=0).reshape(bsz, seq, d)


def kernel(x, norm_mix_g, w_in, qn_g, kn_g, rel_bias, lb_logits, hgrn_norm_g, w_gate, b_gate,
           w_br_a, w_br_b, w_br_c, w_out, norm_ffn_g, w_up, w_down):
    return _block(x, norm_mix_g, w_in, qn_g, kn_g, rel_bias, lb_logits, hgrn_norm_g, w_gate, b_gate,
                  w_br_a, w_br_b, w_br_c, w_out, norm_ffn_g, w_up, w_down,
                  tm_proj=1024, tn_gates=768, tm_merge=256, tm_mlp=1024, tf_mlp=1024)
```

```python
import functools

import numpy as np
import jax
import jax.numpy as jnp
from jax import lax
from jax.experimental import pallas as pl
from jax.experimental.pallas import tpu as pltpu

D_MODEL = 2048
CHUNK = 64
EPS = 1e-6

A_HEADS = 8
A_HEAD_DIM = 128
A_WIDTH = A_HEADS * A_HEAD_DIM
A_PAST_CHUNKS = 8
REL_CLIP = 256

B_HEADS = 8
B_KEY_DIM = 128
B_VAL_DIM = 128
B_WIDTH = B_HEADS * B_KEY_DIM

C_HEADS = 8
C_QK_DIM = 64
C_V_DIM = 128
C_QK_WIDTH = C_HEADS * C_QK_DIM
C_V_WIDTH = C_HEADS * C_V_DIM
ROPE_BASE = 10000.0

N_BRANCH = 3

LANES = 128
VMEM_LIMIT = 60 * 1024 * 1024

BF16 = jnp.bfloat16
F32 = jnp.float32
NT_DIMS = (((1,), (1,)), ((), ()))
TN_DIMS = (((0,), (0,)), ((), ()))
LOG2E = float(np.log2(np.e))


def _params(*semantics):
    return pltpu.CompilerParams(dimension_semantics=semantics, vmem_limit_bytes=VMEM_LIMIT)


def _rms(t, gain=None):
    y = t * lax.rsqrt(jnp.mean(t * t, axis=-1, keepdims=True) + EPS)
    return y if gain is None else y * gain


def _silu(t):
    return t / (1.0 + jnp.exp(-t))


def _neg_abs(t):
    bits = lax.bitcast_convert_type(t, jnp.uint32) | jnp.uint32(0x80000000)
    return lax.bitcast_convert_type(bits, F32)


def _norm_matmul_kernel(x_ref, g_ref, w_ref, o_ref, h_ref):
    @pl.when(pl.program_id(1) == 0)
    def _():
        h_ref[...] = _rms(x_ref[...], g_ref[...]).astype(BF16)

    o_ref[...] = jnp.dot(h_ref[...], w_ref[...], preferred_element_type=F32).astype(o_ref.dtype)


def _norm_matmul(x, g, w, layer, *, n_tiles, tile_of, out_dtype, tm, tn, name):
    s, d = x.shape
    return pl.pallas_call(
        _norm_matmul_kernel,
        grid=(s // tm, n_tiles),
        in_specs=[
            pl.BlockSpec((tm, d), lambda i, j: (i, 0)),
            pl.BlockSpec((None, 1, d), lambda i, j: (layer, 0, 0)),
            pl.BlockSpec((None, d, tn), lambda i, j: (layer, 0, tile_of(j))),
        ],
        out_specs=pl.BlockSpec((tm, tn), lambda i, j: (i, j)),
        out_shape=jax.ShapeDtypeStruct((s, n_tiles * tn), out_dtype),
        scratch_shapes=[pltpu.VMEM((tm, d), BF16)],
        compiler_params=_params("parallel", "arbitrary"),
        name=name,
    )(x, g, w)


A_GROUP = A_PAST_CHUNKS * CHUNK
A_SUB = 4 * CHUNK
A_SUB_WINDOW = A_SUB + A_PAST_CHUNKS * CHUNK
A_STEP = 2 * A_GROUP


def _attn_kernel(q_ref, k_ref, v_ref, bias_ref, gq_ref, gk_ref, o_ref, kbuf, vbuf):
    i = pl.program_id(1)

    @pl.when(i == 0)
    def _():
        kbuf[0:A_GROUP, :] = jnp.zeros((A_GROUP, A_HEAD_DIM), BF16)
        vbuf[0:A_GROUP, :] = jnp.zeros((A_GROUP, A_HEAD_DIM), BF16)

    @pl.when(i > 0)
    def _():
        kbuf[0:A_GROUP, :] = kbuf[A_STEP:, :]
        vbuf[0:A_GROUP, :] = vbuf[A_STEP:, :]

    kbuf[A_GROUP:, :] = _rms(k_ref[...], gk_ref[...]).astype(BF16)
    vbuf[A_GROUP:, :] = v_ref[...]
    q = (_rms(q_ref[...], gq_ref[...]) * (A_HEAD_DIM ** -0.5 * LOG2E)).astype(BF16)

    def attend(first_step):
        subs = range(0, A_STEP, A_SUB)
        scores = []
        for lo in subs:
            s = lax.dot_general(q[lo:lo + A_SUB, :], kbuf[lo:lo + A_SUB_WINDOW, :], NT_DIMS,
                                preferred_element_type=F32) + bias_ref[...]
            if first_step and lo < A_GROUP:
                col = lax.broadcasted_iota(jnp.int32, s.shape, 1)
                s = jnp.where(col >= A_GROUP - lo, s, -jnp.inf)
            scores.append(s)
        probs = []
        for s in scores:
            p = jnp.exp2(s - jnp.max(s, axis=-1, keepdims=True))
            probs.append((p.astype(BF16), jnp.sum(p, axis=-1, keepdims=True)))
        for lo, (p, l) in zip(subs, probs):
            o = jnp.dot(p, vbuf[lo:lo + A_SUB_WINDOW, :], preferred_element_type=F32)
            o_ref[lo:lo + A_SUB, :] = (o / l).astype(o_ref.dtype)

    pl.when(i == 0)(functools.partial(attend, True))
    pl.when(i > 0)(functools.partial(attend, False))


def _attention(pf, pb, bias, gq, gk, layer, *, q_col, k_col, v_col):
    s = pf.shape[0]
    blk = (A_STEP, A_HEAD_DIM)
    cur = lambda c: (lambda h, i: (i, c + h))
    gain_spec = pl.BlockSpec((None, 1, A_HEAD_DIM), lambda h, i: (layer, 0, 0))
    return pl.pallas_call(
        _attn_kernel,
        grid=(A_HEADS, s // blk[0]),
        in_specs=[
            pl.BlockSpec(blk, cur(q_col)),
            pl.BlockSpec(blk, cur(k_col)),
            pl.BlockSpec(blk, cur(v_col)),
            pl.BlockSpec((None, None, A_SUB, A_SUB_WINDOW), lambda h, i: (layer, h, 0, 0)),
            gain_spec, gain_spec,
        ],
        out_specs=pl.BlockSpec(blk, lambda h, i: (i, h)),
        out_shape=jax.ShapeDtypeStruct((s, A_WIDTH), BF16),
        scratch_shapes=[pltpu.VMEM((A_GROUP + A_STEP, A_HEAD_DIM), BF16),
                        pltpu.VMEM((A_GROUP + A_STEP, A_HEAD_DIM), BF16)],
        compiler_params=_params("arbitrary", "arbitrary"),
        name="attention",
    )(pf, pf, pb, bias, gq, gk)


def _rel_bias_window(rel_table):
    t = rel_table.astype(F32)
    lead = t.shape[:-1]
    past = A_PAST_CHUNKS * CHUNK
    n = A_SUB + A_SUB_WINDOW
    n_far = past + A_SUB - 1 - REL_CLIP
    p = jnp.concatenate([jnp.broadcast_to(t[..., 2 * REL_CLIP:], lead + (n_far,)),
                         t[..., ::-1][..., :n - 1 - n_far],
                         jnp.zeros(lead + (1,), F32)], axis=-1)
    v = jnp.roll(p, -(A_SUB - 1), axis=-1)
    rows = jnp.tile(v, A_SUB)[..., :A_SUB * (n - 1)].reshape(lead + (A_SUB, n - 1))
    bias = rows[..., :A_SUB_WINDOW]
    q_chunk = np.arange(A_SUB)[:, None] // CHUNK
    k_chunk = np.arange(A_SUB_WINDOW)[None, :] // CHUNK
    band = (k_chunk >= q_chunk) & (k_chunk <= q_chunk + A_PAST_CHUNKS)
    return jnp.where(band, bias * LOG2E, -jnp.inf)


B_LEVELS = (32, 16, 8, 4, 2, 1)


def _pair_level_matrix():
    t = np.arange(CHUNK)[:, None]
    s = np.arange(CHUNK)[None, :]
    lvl = np.full((CHUNK, CHUNK), -1, np.int32)
    lvl[t == s] = 0
    for n, h in enumerate(B_LEVELS):
        m = (t // (2 * h) == s // (2 * h)) & (t % (2 * h) >= h) & (s % (2 * h) < h)
        lvl[m] = n + 1
    return lvl


def _hgrn_chunk_prep(z, qin, log2lb, log2_1mlb, one_m_lb):
    row = lax.broadcasted_iota(jnp.int32, (CHUNK, B_KEY_DIM), 0)
    z2 = z * LOG2E
    e = jnp.exp2(_neg_abs(z2))
    log2_sig = jnp.minimum(z2, 0.0) - jnp.log2(1.0 + e)
    bb = log2_1mlb + log2_sig
    lf = jnp.maximum(log2lb, bb) + jnp.log2(1.0 + jnp.exp2(_neg_abs(log2lb - bb)))
    kb = (one_m_lb * (jnp.where(z >= 0.0, e, 1.0) / (1.0 + e))).astype(BF16)
    qb = _silu(qin).astype(BF16)

    b = lf
    for sh in (1, 2, 4):
        b = b + jnp.where(row >= sh, pltpu.roll(b, sh, 0), 0.0)
    for sh in (8, 16, 32):
        b = b + jnp.concatenate([jnp.zeros((sh, B_KEY_DIM), F32), b[:CHUNK - sh, :]], axis=0)

    operands = [(qb, kb)]
    for h in B_LEVELS:
        if h >= 4:
            bound = jnp.concatenate(
                [jnp.broadcast_to(b[blk + h - 1:blk + h, :], (2 * h, B_KEY_DIM))
                 for blk in range(0, CHUNK, 2 * h)], axis=0)
            expo = _neg_abs(b - bound)
        elif h == 2:
            r4 = row % 4
            expo = jnp.where(r4 == 0, pltpu.roll(lf, CHUNK - 1, 0),
                             jnp.where(r4 == 1, 0.0,
                                       jnp.where(r4 == 2, lf, lf + pltpu.roll(lf, 1, 0))))
        else:
            expo = jnp.where(row % 2 == 1, lf, 0.0)
        w = jnp.exp2(expo).astype(BF16)
        operands.append((qb * w, kb * w))

    b_last = b[CHUNK - 1:CHUNK, :]
    q_dec = qb * jnp.exp2(b).astype(BF16)
    k_dec = kb * jnp.exp2(b_last - b).astype(BF16)
    return operands, q_dec, k_dec, jnp.exp2(b_last)


def _hgrn_chunk_products(prep, v, level, state_t):
    operands, q_dec, k_dec, chunk_decay = prep
    attn = 0.0
    for n, (qw, kw) in enumerate(operands):
        attn = jnp.where(level == n, lax.dot_general(qw, kw, NT_DIMS, preferred_element_type=F32), attn)
    o_past = lax.dot_general(q_dec, state_t.astype(BF16), NT_DIMS, preferred_element_type=F32)
    new_state_t = chunk_decay * state_t + lax.dot_general(v, k_dec, TN_DIMS, preferred_element_type=F32)
    return attn.astype(BF16), o_past, new_state_t


def _hgrn_chunk_out(attn, o_past, v, g, gain):
    o = o_past + jnp.dot(attn, v, preferred_element_type=F32)
    return _rms(o, gain) * _silu(g)


GATE_PIECE = 256


def _gates_mixers_kernel(x_ref, gm_ref, wg_ref, bg_ref, z_ref, q_ref, v_ref, g_ref, log2lb_ref, log2_1mlb_ref,
                         omlb_ref, gain_ref, level_ref, cq_ref, ck_ref, cv_ref, cg_ref, cos_ref, sin_ref,
                         gates_ref, yb_ref, yc_ref, h_ref, state_ref, ret_state_ref, ret_decay_ref):
    first_col = pl.program_id(1) == 0

    @pl.when(jnp.logical_and(pl.program_id(0) == 0, first_col))
    def _():
        state_ref[...] = jnp.zeros_like(state_ref)
        _retention_init(ret_state_ref, ret_decay_ref)

    @pl.when(first_col)
    def _():
        h_ref[...] = _rms(x_ref[...], gm_ref[...]).astype(BF16)

    level = level_ref[...]
    gain = gain_ref[...]
    head_cols = [slice(h * B_KEY_DIM, (h + 1) * B_KEY_DIM) for h in range(B_HEADS)]

    def gate_piece(p):
        cols = slice(p * GATE_PIECE, (p + 1) * GATE_PIECE)
        acc = jnp.dot(h_ref[...], wg_ref[:, cols], preferred_element_type=F32) + bg_ref[:, cols]
        gates_ref[:, cols] = jax.nn.sigmoid(acc)

    staged = {}

    def prep(c):
        rows = slice(c * CHUNK, (c + 1) * CHUNK)
        staged[c] = [_hgrn_chunk_prep(z_ref[rows, cols], q_ref[rows, cols], log2lb_ref[:, cols],
                                      log2_1mlb_ref[:, cols], omlb_ref[:, cols]) for cols in head_cols]

    def products(c):
        rows = slice(c * CHUNK, (c + 1) * CHUNK)
        for h, cols in enumerate(head_cols):
            attn, o_past, st = _hgrn_chunk_products(staged[c][h], v_ref[rows, cols], level, state_ref[h])
            state_ref[h] = st
            staged[c][h] = (attn, o_past)

    def out(c):
        rows = slice(c * CHUNK, (c + 1) * CHUNK)
        for (attn, o_past), cols in zip(staged[c], head_cols):
            y = _hgrn_chunk_out(attn, o_past, v_ref[rows, cols], g_ref[rows, cols], gain)
            yb_ref[rows, cols] = y.astype(yb_ref.dtype)

    n_chunks = z_ref.shape[0] // CHUNK
    n_pieces = gates_ref.shape[1] // GATE_PIECE
    assert n_pieces > n_chunks
    pieces = iter(range(n_pieces))
    prep(0)
    gate_piece(next(pieces))
    for c in range(n_chunks):
        products(c)
        if c + 1 < n_chunks:
            prep(c + 1)
        if c == 0:
            ret_scores = _retention_scores(cq_ref, ck_ref, cv_ref, cos_ref, sin_ref, ret_state_ref, ret_decay_ref)
        gate_piece(next(pieces))
        out(c)
    for p in pieces:
        gate_piece(p)
    _retention_out(ret_scores, cv_ref, cg_ref, yc_ref)


def _gates_mixers(x, gm, wg, bg, pf, pb, bound_consts, gain, cos, sin, layer, *, z_col, q_col, g_col, v_col,
                  cq_col, ck_col, cg_col, cv_col, tm, tn):
    s, d = x.shape
    n_gate = wg.shape[-1]
    n_col = n_gate // tn
    tb = tm // n_col
    assert tb % CHUNK == 0 and tb * n_col == tm and tn % GATE_PIECE == 0
    level = jnp.asarray(_pair_level_matrix())
    assert B_WIDTH == C_V_WIDTH
    wide = lambda c: pl.BlockSpec((tb, B_WIDTH), lambda i, j: (i * n_col + j, c))
    narrow = lambda c: pl.BlockSpec((tb, C_QK_WIDTH), lambda i, j: (i * n_col + j, c))
    table = pl.BlockSpec((tb, LANES), lambda i, j: (i * n_col + j, 0))
    row = pl.BlockSpec((None, 1, B_WIDTH), lambda i, j: (layer, 0, 0))
    seq_out = pl.BlockSpec((tb, B_WIDTH), lambda i, j: (i * n_col + j, 0))
    return pl.pallas_call(
        _gates_mixers_kernel,
        grid=(s // tm, n_col),
        in_specs=[pl.BlockSpec((tm, d), lambda i, j: (i, 0)),
                  pl.BlockSpec((None, 1, d), lambda i, j: (layer, 0, 0)),
                  pl.BlockSpec((None, d, tn), lambda i, j: (layer, 0, j)),
                  pl.BlockSpec((None, 1, tn), lambda i, j: (layer, 0, j)),
                  wide(z_col), wide(q_col), wide(v_col), wide(g_col), row, row, row,
                  pl.BlockSpec((None, 1, B_VAL_DIM), lambda i, j: (layer, 0, 0)),
                  pl.BlockSpec((CHUNK, CHUNK), lambda i, j: (0, 0)),
                  narrow(cq_col), narrow(ck_col), wide(cv_col), wide(cg_col), table, table],
        out_specs=[pl.BlockSpec((tm, tn), lambda i, j: (i, j)), seq_out, seq_out],
        out_shape=[jax.ShapeDtypeStruct((s, n_gate), F32), jax.ShapeDtypeStruct((s, B_WIDTH), BF16),
                   jax.ShapeDtypeStruct((s, C_V_WIDTH), BF16)],
        scratch_shapes=[pltpu.VMEM((tm, d), BF16), pltpu.VMEM((B_HEADS, B_VAL_DIM, B_KEY_DIM), F32),
                        pltpu.VMEM((C_HEADS, C_QK_DIM, C_V_DIM), F32), pltpu.VMEM((C_HEADS, tb, tb), F32)],
        compiler_params=_params("arbitrary", "arbitrary"),
        name="gates_mixers",
    )(x, gm, wg, bg, pf, pf, pb, pf, *bound_consts, gain, level, pf, pf, pb, pf, cos, sin)


def _hgrn_bound_consts(lb_logits):
    lb_cum = jnp.cumsum(jax.nn.softmax(lb_logits.astype(F32), axis=0), axis=0)
    lb = (lb_cum - lb_cum[0:1])[:, None, :]
    return jnp.log2(lb), jnp.log1p(-lb) * LOG2E, 1.0 - lb


C_LOG_GAMMA = [np.float32(np.log(1.0 - 2.0 ** (-5.0 - h))) for h in range(C_HEADS)]


def _retention_init(state_ref, decay_ref):
    t = decay_ref.shape[-1]
    state_ref[...] = jnp.zeros_like(state_ref)
    ti = lax.broadcasted_iota(jnp.int32, (t, t), 0)
    si = lax.broadcasted_iota(jnp.int32, (t, t), 1)
    rel = (ti - si).astype(F32)
    for h in range(C_HEADS):
        decay_ref[h] = jnp.where(rel >= 0.0, jnp.exp(C_LOG_GAMMA[h] * jnp.maximum(rel, 0.0)), 0.0)


def _retention_scores(q_ref, k_ref, v_ref, cos_ref, sin_ref, state_ref, decay_ref):
    t = q_ref.shape[0]
    cos = cos_ref[...]
    sin = sin_ref[...]
    lane = lax.broadcasted_iota(jnp.int32, (t, LANES), 1)
    first_half = (lane % C_QK_DIM) < (C_QK_DIM // 2)

    def rotary(x):
        swapped = jnp.where(first_half, pltpu.roll(x, LANES - C_QK_DIM // 2, 1),
                            pltpu.roll(x, C_QK_DIM // 2, 1))
        return x * cos + swapped * sin

    pos = lax.broadcasted_iota(jnp.int32, (t, C_QK_DIM), 0).astype(F32)
    partial_out = []
    for pair in range(C_HEADS // 2):
        lanes = slice(pair * LANES, (pair + 1) * LANES)
        q2 = rotary(q_ref[:, lanes])
        k2 = rotary(k_ref[:, lanes]) * (C_QK_DIM ** -0.5)
        for sub in range(2):
            h = 2 * pair + sub
            qh = q2[:, sub * C_QK_DIM:(sub + 1) * C_QK_DIM]
            kh = k2[:, sub * C_QK_DIM:(sub + 1) * C_QK_DIM]
            vh = v_ref[:, h * C_V_DIM:(h + 1) * C_V_DIM]
            scores = lax.dot_general(qh.astype(BF16), kh.astype(BF16), NT_DIMS,
                                     preferred_element_type=F32) * decay_ref[h]
            state = state_ref[h]
            q_dec = qh * jnp.exp(C_LOG_GAMMA[h] * (pos + 1.0))
            o_past = jnp.dot(q_dec.astype(BF16), state.astype(BF16), preferred_element_type=F32)
            k_dec = kh * jnp.exp(C_LOG_GAMMA[h] * (t - 1.0 - pos))
            state_ref[h] = (np.float32(np.exp(C_LOG_GAMMA[h] * t)) * state
                            + lax.dot_general(k_dec.astype(BF16), vh, TN_DIMS, preferred_element_type=F32))
            partial_out.append((scores.astype(BF16), o_past))
    return partial_out


def _retention_out(partial_out, v_ref, g_ref, o_ref):
    for h, (scores, o_past) in enumerate(partial_out):
        cols = slice(h * C_V_DIM, (h + 1) * C_V_DIM)
        o = o_past + jnp.dot(scores, v_ref[:, cols], preferred_element_type=F32)
        o_ref[:, cols] = (_rms(o) * _silu(g_ref[:, cols])).astype(o_ref.dtype)


def _rotary_tables(seq):
    half = C_QK_DIM // 2
    inv_freq = jnp.asarray(1.0 / ROPE_BASE ** np.linspace(0.0, 1.0, half), F32)
    ang = jnp.arange(seq).astype(F32)[:, None] * inv_freq[None, :]
    cos, sin = jnp.cos(ang), jnp.sin(ang)
    reps = LANES // C_QK_DIM
    return (jnp.tile(jnp.concatenate([cos, cos], axis=1), (1, reps)),
            jnp.tile(jnp.concatenate([-sin, sin], axis=1), (1, reps)))


def _merge_kernel(ya_ref, yb_ref, yc_ref, ga_ref, gb_ref, gc_ref, x_ref, wa_ref, wb_ref, wc_ref,
                  wo_ref, xo_ref):
    merged = ga_ref[...] * jnp.dot(ya_ref[...], wa_ref[...], preferred_element_type=F32)
    merged = merged + gb_ref[...] * jnp.dot(yb_ref[...], wb_ref[...], preferred_element_type=F32)
    merged = merged + gc_ref[...] * jnp.dot(yc_ref[...], wc_ref[...], preferred_element_type=F32)
    xo_ref[...] = x_ref[...] + jnp.dot(merged.astype(BF16), wo_ref[...], preferred_element_type=F32)


def _merge_out(ya, yb, yc, gates, x, wa, wb, wc, wo, layer, *, tm):
    s, d = x.shape
    once = pl.Buffered(1)
    y_spec = pl.BlockSpec((tm, A_WIDTH), lambda i: (i, 0))
    gate_spec = lambda c: pl.BlockSpec((tm, d), lambda i: (i, c))
    w_spec = pl.BlockSpec((None, A_WIDTH, d), lambda i: (layer, 0, 0), pipeline_mode=once)
    return pl.pallas_call(
        _merge_kernel,
        grid=(s // tm,),
        in_specs=[y_spec, y_spec, y_spec, gate_spec(0), gate_spec(1), gate_spec(2),
                  pl.BlockSpec((tm, d), lambda i: (i, 0)),
                  w_spec, w_spec, w_spec,
                  pl.BlockSpec((None, d, d), lambda i: (layer, 0, 0), pipeline_mode=once)],
        out_specs=pl.BlockSpec((tm, d), lambda i: (i, 0)),
        out_shape=jax.ShapeDtypeStruct((s, d), F32),
        compiler_params=_params("parallel"),
        name="merge_out",
    )(ya, yb, yc, gates, gates, gates, x, wa, wb, wc, wo)


def _mlp_kernel(x_ref, g_ref, wu_ref, wd_ref, o_ref, h_ref):
    @pl.when(pl.program_id(1) == 0)
    def _():
        x = x_ref[...]
        h_ref[...] = _rms(x, g_ref[...]).astype(BF16)
        o_ref[...] = x

    up = jnp.maximum(jnp.dot(h_ref[...], wu_ref[...], preferred_element_type=F32), 0.0)
    o_ref[...] += jnp.dot((up * up).astype(BF16), wd_ref[...], preferred_element_type=F32)


def _mlp(x, g, wu, wd, layer, *, tm, tf):
    s, d = x.shape
    ff = wu.shape[-1]
    return pl.pallas_call(
        _mlp_kernel,
        grid=(s // tm, ff // tf),
        in_specs=[pl.BlockSpec((tm, d), lambda i, f: (i, 0)),
                  pl.BlockSpec((None, 1, d), lambda i, f: (layer, 0, 0)),
                  pl.BlockSpec((None, d, tf), lambda i, f: (layer, 0, f)),
                  pl.BlockSpec((None, tf, d), lambda i, f: (layer, f, 0))],
        out_specs=pl.BlockSpec((tm, d), lambda i, f: (i, 0)),
        out_shape=jax.ShapeDtypeStruct((s, d), F32),
        scratch_shapes=[pltpu.VMEM((tm, d), BF16)],
        compiler_params=_params("parallel", "arbitrary"),
        name="mlp",
    )(x, g, wu, wd)


PROJ_TILE = 1024
F32_TILES = (0, 1, 3, 4, 6, 7, 9)
BF16_TILES = (2, 5, 8)


def _f32_tile_of(j):
    return j + (j >= 2).astype(jnp.int32) + (j >= 4).astype(jnp.int32) + (j >= 6).astype(jnp.int32)


def _bf16_tile_of(j):
    return 3 * j + 2


def _layer(x, layer, w, consts, *, tm_proj, tn_gates, tm_merge, tm_mlp, tf_mlp):
    pf = _norm_matmul(x, w["norm_mix_g"], w["w_in"], layer, n_tiles=len(F32_TILES), tile_of=_f32_tile_of,
                      out_dtype=F32, tm=tm_proj, tn=PROJ_TILE, name="proj_f32")
    pb = _norm_matmul(x, w["norm_mix_g"], w["w_in"], layer, n_tiles=len(BF16_TILES), tile_of=_bf16_tile_of,
                      out_dtype=BF16, tm=tm_proj, tn=PROJ_TILE, name="proj_bf16")
    gates, yb, yc = _gates_mixers(x, w["norm_mix_g"], w["w_gate"], w["b_gate"], pf, pb, consts["hgrn_bounds"],
                                  w["hgrn_norm_g"], consts["cos"], consts["sin"], layer,
                                  z_col=2, q_col=3, g_col=4, v_col=1, cq_col=5120 // C_QK_WIDTH,
                                  ck_col=5632 // C_QK_WIDTH, cg_col=6144 // C_V_WIDTH, cv_col=2,
                                  tm=tm_proj, tn=tn_gates)

    ya = _attention(pf, pb, consts["rel_bias"], w["qn_g"], w["kn_g"], layer,
                    q_col=0, k_col=A_WIDTH // A_HEAD_DIM, v_col=0)

    x_new = _merge_out(ya, yb, yc, gates, x, w["w_br_a"], w["w_br_b"], w["w_br_c"], w["w_out"], layer,
                       tm=tm_merge)
    return _mlp(x_new, w["norm_ffn_g"], w["w_up"], w["w_down"], layer, tm=tm_mlp, tf=tf_mlp)


def _block(x, norm_mix_g, w_in, qn_g, kn_g, rel_bias, lb_logits, hgrn_norm_g, w_gate, b_gate,
           w_br_a, w_br_b, w_br_c, w_out, norm_ffn_g, w_up, w_down, **tiles):
    bsz, seq, d = x.shape
    depth = w_in.shape[0]
    row = lambda t: t.astype(F32)[:, None, :]
    w = dict(norm_mix_g=row(norm_mix_g), w_in=w_in.astype(BF16), qn_g=row(qn_g), kn_g=row(kn_g),
             hgrn_norm_g=row(hgrn_norm_g), w_gate=w_gate.astype(BF16), b_gate=row(b_gate),
             w_br_a=w_br_a.astype(BF16), w_br_b=w_br_b.astype(BF16), w_br_c=w_br_c.astype(BF16),
             w_out=w_out.astype(BF16), norm_ffn_g=row(norm_ffn_g), w_up=w_up.astype(BF16),
             w_down=w_down.astype(BF16))
    cos, sin = _rotary_tables(seq)
    consts = dict(rel_bias=_rel_bias_window(rel_bias), hgrn_bounds=_hgrn_bound_consts(lb_logits),
                  cos=cos, sin=sin)
    rows = x.reshape(bsz * seq, d)
    outs = []
    for b in range(bsz):
        xb = rows[b * seq:(b + 1) * seq]
        for layer in range(depth):
            xb = _layer(xb, layer, w, consts, **tiles)
        outs.append(xb)
    return jnp.concatenate(outs, axis=0).reshape(bsz, seq, d)


def kernel(x, norm_mix_g, w_in, qn_g, kn_g, rel_bias, lb_logits, hgrn_norm_g, w_gate, b_gate,
           w_br_a, w_br_b, w_br_c, w_out, norm_ffn_g, w_up, w_down):
    return _block(x, norm_mix_g, w_in, qn_g, kn_g, rel_bias, lb_logits, hgrn_norm_g, w_gate, b_gate,
                  w_br_a, w_br_b, w_br_c, w_out, norm_ffn_g, w_up, w_down,
                  tm_proj=1024, tn_gates=768, tm_merge=256, tm_mlp=1024, tf_mlp=1024)
```

```python
import functools

import numpy as np
import jax
import jax.numpy as jnp
from jax import lax
from jax.experimental import pallas as pl
from jax.experimental.pallas import tpu as pltpu

D_MODEL = 2048
CHUNK = 64
EPS = 1e-6

A_HEADS = 8
A_HEAD_DIM = 128
A_WIDTH = A_HEADS * A_HEAD_DIM
A_PAST_CHUNKS = 8
REL_CLIP = 256

B_HEADS = 8
B_KEY_DIM = 128
B_VAL_DIM = 128
B_WIDTH = B_HEADS * B_KEY_DIM

C_HEADS = 8
C_QK_DIM = 64
C_V_DIM = 128
C_QK_WIDTH = C_HEADS * C_QK_DIM
C_V_WIDTH = C_HEADS * C_V_DIM
ROPE_BASE = 10000.0

N_BRANCH = 3

LANES = 128
VMEM_LIMIT = 60 * 1024 * 1024

BF16 = jnp.bfloat16
F32 = jnp.float32
NT_DIMS = (((1,), (1,)), ((), ()))
TN_DIMS = (((0,), (0,)), ((), ()))
LOG2E = float(np.log2(np.e))


def _params(*semantics):
    return pltpu.CompilerParams(dimension_semantics=semantics, vmem_limit_bytes=VMEM_LIMIT)


def _rms(t, gain=None):
    y = t * lax.rsqrt(jnp.mean(t * t, axis=-1, keepdims=True) + EPS)
    return y if gain is None else y * gain


def _silu(t):
    return t / (1.0 + jnp.exp(-t))


def _neg_abs(t):
    bits = lax.bitcast_convert_type(t, jnp.uint32) | jnp.uint32(0x80000000)
    return lax.bitcast_convert_type(bits, F32)


PROJ_TILE = 1024
F32_TILES = (0, 1, 3, 4, 6, 7, 9)
BF16_TILES = (2, 5, 8)


def _proj_tile_of(j):
    n = len(F32_TILES)
    f32_tile = j + (j >= 2).astype(jnp.int32) + (j >= 4).astype(jnp.int32) + (j >= 6).astype(jnp.int32)
    return jnp.where(j < n, f32_tile, 3 * (j - n) + 2)


def _proj_kernel(x_ref, g_ref, w_ref, pf_ref, pb_ref, h_ref):
    j = pl.program_id(1)

    @pl.when(j == 0)
    def _():
        h_ref[...] = _rms(x_ref[...], g_ref[...]).astype(BF16)

    @pl.when(j < len(F32_TILES))
    def _():
        pf_ref[...] = jnp.dot(h_ref[...], w_ref[...], preferred_element_type=F32)

    @pl.when(j >= len(F32_TILES))
    def _():
        pb_ref[...] = jnp.dot(h_ref[...], w_ref[...], preferred_element_type=F32).astype(BF16)


def _proj(x, g, w, layer, *, tm):
    s, d = x.shape
    n_f32, n_bf16 = len(F32_TILES), len(BF16_TILES)
    return pl.pallas_call(
        _proj_kernel,
        grid=(s // tm, n_f32 + n_bf16),
        in_specs=[
            pl.BlockSpec((tm, d), lambda i, j: (i, 0)),
            pl.BlockSpec((None, 1, d), lambda i, j: (layer, 0, 0)),
            pl.BlockSpec((None, d, PROJ_TILE), lambda i, j: (layer, 0, _proj_tile_of(j))),
        ],
        out_specs=[pl.BlockSpec((tm, PROJ_TILE), lambda i, j: (i, jnp.minimum(j, n_f32 - 1))),
                   pl.BlockSpec((tm, PROJ_TILE), lambda i, j: (i, jnp.maximum(j - n_f32, 0)))],
        out_shape=[jax.ShapeDtypeStruct((s, n_f32 * PROJ_TILE), F32),
                   jax.ShapeDtypeStruct((s, n_bf16 * PROJ_TILE), BF16)],
        scratch_shapes=[pltpu.VMEM((tm, d), BF16)],
        compiler_params=_params("arbitrary", "arbitrary"),
        name="proj",
    )(x, g, w)


A_GROUP = A_PAST_CHUNKS * CHUNK
A_SUB = 4 * CHUNK
A_SUB_WINDOW = A_SUB + A_PAST_CHUNKS * CHUNK
A_STEP = 4 * A_GROUP


def _attn_kernel(q_ref, k_ref, v_ref, bias_ref, gq_ref, gk_ref, o_ref, kbuf, vbuf):
    i = pl.program_id(1)

    @pl.when(i == 0)
    def _():
        kbuf[0:A_GROUP, :] = jnp.zeros((A_GROUP, A_HEAD_DIM), BF16)
        vbuf[0:A_GROUP, :] = jnp.zeros((A_GROUP, A_HEAD_DIM), BF16)

    @pl.when(i > 0)
    def _():
        kbuf[0:A_GROUP, :] = kbuf[A_STEP:, :]
        vbuf[0:A_GROUP, :] = vbuf[A_STEP:, :]

    kbuf[A_GROUP:, :] = _rms(k_ref[...], gk_ref[...]).astype(BF16)
    vbuf[A_GROUP:, :] = v_ref[...]
    q = (_rms(q_ref[...], gq_ref[...]) * (A_HEAD_DIM ** -0.5 * LOG2E)).astype(BF16)

    def attend(first_step):
        subs = range(0, A_STEP, A_SUB)
        scores = []
        for lo in subs:
            s = lax.dot_general(q[lo:lo + A_SUB, :], kbuf[lo:lo + A_SUB_WINDOW, :], NT_DIMS,
                                preferred_element_type=F32) + bias_ref[...]
            if first_step and lo < A_GROUP:
                col = lax.broadcasted_iota(jnp.int32, s.shape, 1)
                s = jnp.where(col >= A_GROUP - lo, s, -jnp.inf)
            scores.append(s)
        probs = []
        for s in scores:
            p = jnp.exp2(s - jnp.max(s, axis=-1, keepdims=True))
            probs.append((p.astype(BF16), jnp.sum(p, axis=-1, keepdims=True)))
        for lo, (p, l) in zip(subs, probs):
            o = jnp.dot(p, vbuf[lo:lo + A_SUB_WINDOW, :], preferred_element_type=F32)
            o_ref[lo:lo + A_SUB, :] = (o / l).astype(o_ref.dtype)

    pl.when(i == 0)(functools.partial(attend, True))
    pl.when(i > 0)(functools.partial(attend, False))


def _attention(pf, pb, bias, gq, gk, layer, *, q_col, k_col, v_col):
    s = pf.shape[0]
    blk = (A_STEP, A_HEAD_DIM)
    cur = lambda c: (lambda h, i: (i, c + h))
    gain_spec = pl.BlockSpec((None, 1, A_HEAD_DIM), lambda h, i: (layer, 0, 0))
    return pl.pallas_call(
        _attn_kernel,
        grid=(A_HEADS, s // blk[0]),
        in_specs=[
            pl.BlockSpec(blk, cur(q_col)),
            pl.BlockSpec(blk, cur(k_col)),
            pl.BlockSpec(blk, cur(v_col)),
            pl.BlockSpec((None, None, A_SUB, A_SUB_WINDOW), lambda h, i: (layer, h, 0, 0)),
            gain_spec, gain_spec,
        ],
        out_specs=pl.BlockSpec(blk, lambda h, i: (i, h)),
        out_shape=jax.ShapeDtypeStruct((s, A_WIDTH), BF16),
        scratch_shapes=[pltpu.VMEM((A_GROUP + A_STEP, A_HEAD_DIM), BF16),
                        pltpu.VMEM((A_GROUP + A_STEP, A_HEAD_DIM), BF16)],
        compiler_params=_params("arbitrary", "arbitrary"),
        name="attention",
    )(pf, pf, pb, bias, gq, gk)


def _rel_bias_window(rel_table):
    t = rel_table.astype(F32)
    lead = t.shape[:-1]
    past = A_PAST_CHUNKS * CHUNK
    n = A_SUB + A_SUB_WINDOW
    n_far = past + A_SUB - 1 - REL_CLIP
    p = jnp.concatenate([jnp.broadcast_to(t[..., 2 * REL_CLIP:], lead + (n_far,)),
                         t[..., ::-1][..., :n - 1 - n_far],
                         jnp.zeros(lead + (1,), F32)], axis=-1)
    v = jnp.roll(p, -(A_SUB - 1), axis=-1)
    rows = jnp.tile(v, A_SUB)[..., :A_SUB * (n - 1)].reshape(lead + (A_SUB, n - 1))
    bias = rows[..., :A_SUB_WINDOW]
    q_chunk = np.arange(A_SUB)[:, None] // CHUNK
    k_chunk = np.arange(A_SUB_WINDOW)[None, :] // CHUNK
    band = (k_chunk >= q_chunk) & (k_chunk <= q_chunk + A_PAST_CHUNKS)
    return jnp.where(band, bias * LOG2E, -jnp.inf)


B_LEVELS = (32, 16, 8, 4, 2, 1)


def _pair_level_matrix():
    t = np.arange(CHUNK)[:, None]
    s = np.arange(CHUNK)[None, :]
    lvl = np.full((CHUNK, CHUNK), -1, np.int32)
    lvl[t == s] = 0
    for n, h in enumerate(B_LEVELS):
        m = (t // (2 * h) == s // (2 * h)) & (t % (2 * h) >= h) & (s % (2 * h) < h)
        lvl[m] = n + 1
    return lvl


def _hgrn_chunk_prep(z, qin, log2lb, log2_1mlb, one_m_lb):
    row = lax.broadcasted_iota(jnp.int32, (CHUNK, B_KEY_DIM), 0)
    z2 = z * LOG2E
    e = jnp.exp2(_neg_abs(z2))
    log2_sig = jnp.minimum(z2, 0.0) - jnp.log2(1.0 + e)
    bb = log2_1mlb + log2_sig
    lf = jnp.maximum(log2lb, bb) + jnp.log2(1.0 + jnp.exp2(_neg_abs(log2lb - bb)))
    kb = (one_m_lb * (jnp.where(z >= 0.0, e, 1.0) / (1.0 + e))).astype(BF16)
    qb = _silu(qin).astype(BF16)

    b = lf
    for sh in (1, 2, 4):
        b = b + jnp.where(row >= sh, pltpu.roll(b, sh, 0), 0.0)
    for sh in (8, 16, 32):
        b = b + jnp.concatenate([jnp.zeros((sh, B_KEY_DIM), F32), b[:CHUNK - sh, :]], axis=0)

    operands = [(qb, kb)]
    for h in B_LEVELS:
        if h >= 4:
            bound = jnp.concatenate(
                [jnp.broadcast_to(b[blk + h - 1:blk + h, :], (2 * h, B_KEY_DIM))
                 for blk in range(0, CHUNK, 2 * h)], axis=0)
            expo = _neg_abs(b - bound)
        elif h == 2:
            r4 = row % 4
            expo = jnp.where(r4 == 0, pltpu.roll(lf, CHUNK - 1, 0),
                             jnp.where(r4 == 1, 0.0,
                                       jnp.where(r4 == 2, lf, lf + pltpu.roll(lf, 1, 0))))
        else:
            expo = jnp.where(row % 2 == 1, lf, 0.0)
        w = jnp.exp2(expo).astype(BF16)
        operands.append((qb * w, kb * w))

    b_last = b[CHUNK - 1:CHUNK, :]
    q_dec = qb * jnp.exp2(b).astype(BF16)
    k_dec = kb * jnp.exp2(b_last - b).astype(BF16)
    return operands, q_dec, k_dec, jnp.exp2(b_last)


def _hgrn_chunk_products(prep, v, level, state_t):
    operands, q_dec, k_dec, chunk_decay = prep
    attn = 0.0
    for n, (qw, kw) in enumerate(operands):
        attn = jnp.where(level == n, lax.dot_general(qw, kw, NT_DIMS, preferred_element_type=F32), attn)
    o_past = lax.dot_general(q_dec, state_t.astype(BF16), NT_DIMS, preferred_element_type=F32)
    new_state_t = chunk_decay * state_t + lax.dot_general(v, k_dec, TN_DIMS, preferred_element_type=F32)
    return attn.astype(BF16), o_past, new_state_t


def _hgrn_chunk_out(attn, o_past, v, g, gain):
    o = o_past + jnp.dot(attn, v, preferred_element_type=F32)
    return _rms(o, gain) * _silu(g)


GATE_PIECE = 256


def _gates_mixers_kernel(x_ref, gm_ref, wg_ref, bg_ref, z_ref, q_ref, v_ref, g_ref, log2lb_ref, log2_1mlb_ref,
                         omlb_ref, gain_ref, level_ref, cq_ref, ck_ref, cv_ref, cg_ref, cos_ref, sin_ref,
                         gates_ref, yb_ref, yc_ref, h_ref, state_ref, ret_state_ref, ret_decay_ref):
    first_col = pl.program_id(1) == 0

    @pl.when(jnp.logical_and(pl.program_id(0) == 0, first_col))
    def _():
        state_ref[...] = jnp.zeros_like(state_ref)
        _retention_init(ret_state_ref, ret_decay_ref)

    @pl.when(first_col)
    def _():
        h_ref[...] = _rms(x_ref[...], gm_ref[...]).astype(BF16)

    level = level_ref[...]
    gain = gain_ref[...]
    head_cols = [slice(h * B_KEY_DIM, (h + 1) * B_KEY_DIM) for h in range(B_HEADS)]

    def gate_piece(p):
        cols = slice(p * GATE_PIECE, (p + 1) * GATE_PIECE)
        acc = jnp.dot(h_ref[...], wg_ref[:, cols], preferred_element_type=F32) + bg_ref[:, cols]
        gates_ref[:, cols] = jax.nn.sigmoid(acc)

    staged = {}

    def prep(c):
        rows = slice(c * CHUNK, (c + 1) * CHUNK)
        staged[c] = [_hgrn_chunk_prep(z_ref[rows, cols], q_ref[rows, cols], log2lb_ref[:, cols],
                                      log2_1mlb_ref[:, cols], omlb_ref[:, cols]) for cols in head_cols]

    def products(c):
        rows = slice(c * CHUNK, (c + 1) * CHUNK)
        for h, cols in enumerate(head_cols):
            attn, o_past, st = _hgrn_chunk_products(staged[c][h], v_ref[rows, cols], level, state_ref[h])
            state_ref[h] = st
            staged[c][h] = (attn, o_past)

    def out(c):
        rows = slice(c * CHUNK, (c + 1) * CHUNK)
        for (attn, o_past), cols in zip(staged[c], head_cols):
            y = _hgrn_chunk_out(attn, o_past, v_ref[rows, cols], g_ref[rows, cols], gain)
            yb_ref[rows, cols] = y.astype(yb_ref.dtype)

    n_chunks = z_ref.shape[0] // CHUNK
    n_pieces = gates_ref.shape[1] // GATE_PIECE
    assert n_pieces > n_chunks
    pieces = iter(range(n_pieces))
    prep(0)
    gate_piece(next(pieces))
    for c in range(n_chunks):
        products(c)
        if c + 1 < n_chunks:
            prep(c + 1)
        if c == 0:
            ret_scores = _retention_scores(cq_ref, ck_ref, cv_ref, cos_ref, sin_ref, ret_state_ref, ret_decay_ref)
        gate_piece(next(pieces))
        out(c)
    for p in pieces:
        gate_piece(p)
    _retention_out(ret_scores, cv_ref, cg_ref, yc_ref)


def _gates_mixers(x, gm, wg, bg, pf, pb, bound_consts, gain, cos, sin, layer, *, z_col, q_col, g_col, v_col,
                  cq_col, ck_col, cg_col, cv_col, tm, tn):
    s, d = x.shape
    n_gate = wg.shape[-1]
    n_col = n_gate // tn
    tb = tm // n_col
    assert tb % CHUNK == 0 and tb * n_col == tm and tn % GATE_PIECE == 0
    level = jnp.asarray(_pair_level_matrix())
    assert B_WIDTH == C_V_WIDTH
    wide = lambda c: pl.BlockSpec((tb, B_WIDTH), lambda i, j: (i * n_col + j, c))
    narrow = lambda c: pl.BlockSpec((tb, C_QK_WIDTH), lambda i, j: (i * n_col + j, c))
    table = pl.BlockSpec((tb, LANES), lambda i, j: (i * n_col + j, 0))
    row = pl.BlockSpec((None, 1, B_WIDTH), lambda i, j: (layer, 0, 0))
    seq_out = pl.BlockSpec((tb, B_WIDTH), lambda i, j: (i * n_col + j, 0))
    return pl.pallas_call(
        _gates_mixers_kernel,
        grid=(s // tm, n_col),
        in_specs=[pl.BlockSpec((tm, d), lambda i, j: (i, 0)),
                  pl.BlockSpec((None, 1, d), lambda i, j: (layer, 0, 0)),
                  pl.BlockSpec((None, d, tn), lambda i, j: (layer, 0, j)),
                  pl.BlockSpec((None, 1, tn), lambda i, j: (layer, 0, j)),
                  wide(z_col), wide(q_col), wide(v_col), wide(g_col), row, row, row,
                  pl.BlockSpec((None, 1, B_VAL_DIM), lambda i, j: (layer, 0, 0)),
                  pl.BlockSpec((CHUNK, CHUNK), lambda i, j: (0, 0)),
                  narrow(cq_col), narrow(ck_col), wide(cv_col), wide(cg_col), table, table],
        out_specs=[pl.BlockSpec((tm, tn), lambda i, j: (i, j)), seq_out, seq_out],
        out_shape=[jax.ShapeDtypeStruct((s, n_gate), F32), jax.ShapeDtypeStruct((s, B_WIDTH), BF16),
                   jax.ShapeDtypeStruct((s, C_V_WIDTH), BF16)],
        scratch_shapes=[pltpu.VMEM((tm, d), BF16), pltpu.VMEM((B_HEADS, B_VAL_DIM, B_KEY_DIM), F32),
                        pltpu.VMEM((C_HEADS, C_QK_DIM, C_V_DIM), F32), pltpu.VMEM((C_HEADS, tb, tb), F32)],
        compiler_params=_params("arbitrary", "arbitrary"),
        name="gates_mixers",
    )(x, gm, wg, bg, pf, pf, pb, pf, *bound_consts, gain, level, pf, pf, pb, pf, cos, sin)


def _hgrn_bound_consts(lb_logits):
    lb_cum = jnp.cumsum(jax.nn.softmax(lb_logits.astype(F32), axis=0), axis=0)
    lb = (lb_cum - lb_cum[0:1])[:, None, :]
    return jnp.log2(lb), jnp.log1p(-lb) * LOG2E, 1.0 - lb


C_LOG_GAMMA = [np.float32(np.log(1.0 - 2.0 ** (-5.0 - h))) for h in range(C_HEADS)]


def _retention_init(state_ref, decay_ref):
    t = decay_ref.shape[-1]
    state_ref[...] = jnp.zeros_like(state_ref)
    ti = lax.broadcasted_iota(jnp.int32, (t, t), 0)
    si = lax.broadcasted_iota(jnp.int32, (t, t), 1)
    rel = (ti - si).astype(F32)
    for h in range(C_HEADS):
        decay_ref[h] = jnp.where(rel >= 0.0, jnp.exp(C_LOG_GAMMA[h] * jnp.maximum(rel, 0.0)), 0.0)


def _retention_scores(q_ref, k_ref, v_ref, cos_ref, sin_ref, state_ref, decay_ref):
    t = q_ref.shape[0]
    cos = cos_ref[...]
    sin = sin_ref[...]
    lane = lax.broadcasted_iota(jnp.int32, (t, LANES), 1)
    first_half = (lane % C_QK_DIM) < (C_QK_DIM // 2)

    def rotary(x):
        swapped = jnp.where(first_half, pltpu.roll(x, LANES - C_QK_DIM // 2, 1),
                            pltpu.roll(x, C_QK_DIM // 2, 1))
        return x * cos + swapped * sin

    pos = lax.broadcasted_iota(jnp.int32, (t, C_QK_DIM), 0).astype(F32)
    partial_out = []
    for pair in range(C_HEADS // 2):
        lanes = slice(pair * LANES, (pair + 1) * LANES)
        q2 = rotary(q_ref[:, lanes])
        k2 = rotary(k_ref[:, lanes]) * (C_QK_DIM ** -0.5)
        for sub in range(2):
            h = 2 * pair + sub
            qh = q2[:, sub * C_QK_DIM:(sub + 1) * C_QK_DIM]
            kh = k2[:, sub * C_QK_DIM:(sub + 1) * C_QK_DIM]
            vh = v_ref[:, h * C_V_DIM:(h + 1) * C_V_DIM]
            scores = lax.dot_general(qh.astype(BF16), kh.astype(BF16), NT_DIMS,
                                     preferred_element_type=F32) * decay_ref[h]
            state = state_ref[h]
            q_dec = qh * jnp.exp(C_LOG_GAMMA[h] * (pos + 1.0))
            o_past = jnp.dot(q_dec.astype(BF16), state.astype(BF16), preferred_element_type=F32)
            k_dec = kh * jnp.exp(C_LOG_GAMMA[h] * (t - 1.0 - pos))
            state_ref[h] = (np.float32(np.exp(C_LOG_GAMMA[h] * t)) * state
                            + lax.dot_general(k_dec.astype(BF16), vh, TN_DIMS, preferred_element_type=F32))
            partial_out.append((scores.astype(BF16), o_past))
    return partial_out


def _retention_out(partial_out, v_ref, g_ref, o_ref):
    for h, (scores, o_past) in enumerate(partial_out):
        cols = slice(h * C_V_DIM, (h + 1) * C_V_DIM)
        o = o_past + jnp.dot(scores, v_ref[:, cols], preferred_element_type=F32)
        o_ref[:, cols] = (_rms(o) * _silu(g_ref[:, cols])).astype(o_ref.dtype)


def _rotary_tables(seq):
    half = C_QK_DIM // 2
    inv_freq = jnp.asarray(1.0 / ROPE_BASE ** np.linspace(0.0, 1.0, half), F32)
    ang = jnp.arange(seq).astype(F32)[:, None] * inv_freq[None, :]
    cos, sin = jnp.cos(ang), jnp.sin(ang)
    reps = LANES // C_QK_DIM
    return (jnp.tile(jnp.concatenate([cos, cos], axis=1), (1, reps)),
            jnp.tile(jnp.concatenate([-sin, sin], axis=1), (1, reps)))


def _merge_kernel(ya_ref, yb_ref, yc_ref, ga_ref, gb_ref, gc_ref, x_ref, wa_ref, wb_ref, wc_ref,
                  wo_ref, xo_ref):
    merged = ga_ref[...] * jnp.dot(ya_ref[...], wa_ref[...], preferred_element_type=F32)
    merged = merged + gb_ref[...] * jnp.dot(yb_ref[...], wb_ref[...], preferred_element_type=F32)
    merged = merged + gc_ref[...] * jnp.dot(yc_ref[...], wc_ref[...], preferred_element_type=F32)
    xo_ref[...] = x_ref[...] + jnp.dot(merged.astype(BF16), wo_ref[...], preferred_element_type=F32)


def _merge_out(ya, yb, yc, gates, x, wa, wb, wc, wo, layer, *, tm):
    s, d = x.shape
    once = pl.Buffered(1)
    y_spec = pl.BlockSpec((tm, A_WIDTH), lambda i: (i, 0))
    gate_spec = lambda c: pl.BlockSpec((tm, d), lambda i: (i, c))
    w_spec = pl.BlockSpec((None, A_WIDTH, d), lambda i: (layer, 0, 0), pipeline_mode=once)
    return pl.pallas_call(
        _merge_kernel,
        grid=(s // tm,),
        in_specs=[y_spec, y_spec, y_spec, gate_spec(0), gate_spec(1), gate_spec(2),
                  pl.BlockSpec((tm, d), lambda i: (i, 0)),
                  w_spec, w_spec, w_spec,
                  pl.BlockSpec((None, d, d), lambda i: (layer, 0, 0), pipeline_mode=once)],
        out_specs=pl.BlockSpec((tm, d), lambda i: (i, 0)),
        out_shape=jax.ShapeDtypeStruct((s, d), F32),
        compiler_params=_params("parallel"),
        name="merge_out",
    )(ya, yb, yc, gates, gates, gates, x, wa, wb, wc, wo)


def _mlp_kernel(x_ref, g_ref, wu_ref, wd_ref, o_ref, h_ref):
    @pl.when(pl.program_id(1) == 0)
    def _():
        x = x_ref[...]
        h_ref[...] = _rms(x, g_ref[...]).astype(BF16)
        o_ref[...] = x

    up = jnp.maximum(jnp.dot(h_ref[...], wu_ref[...], preferred_element_type=F32), 0.0)
    o_ref[...] += jnp.dot((up * up).astype(BF16), wd_ref[...], preferred_element_type=F32)


def _mlp(x, g, wu, wd, layer, *, tm, tf):
    s, d = x.shape
    ff = wu.shape[-1]
    return pl.pallas_call(
        _mlp_kernel,
        grid=(s // tm, ff // tf),
        in_specs=[pl.BlockSpec((tm, d), lambda i, f: (i, 0)),
                  pl.BlockSpec((None, 1, d), lambda i, f: (layer, 0, 0)),
                  pl.BlockSpec((None, d, tf), lambda i, f: (layer, 0, f)),
                  pl.BlockSpec((None, tf, d), lambda i, f: (layer, f, 0))],
        out_specs=pl.BlockSpec((tm, d), lambda i, f: (i, 0)),
        out_shape=jax.ShapeDtypeStruct((s, d), F32),
        scratch_shapes=[pltpu.VMEM((tm, d), BF16)],
        compiler_params=_params("parallel", "arbitrary"),
        name="mlp",
    )(x, g, wu, wd)


def _layer(x, layer, w, consts, *, tm_proj, tn_gates, tm_merge, tm_mlp, tf_mlp):
    pf, pb = _proj(x, w["norm_mix_g"], w["w_in"], layer, tm=tm_proj)
    gates, yb, yc = _gates_mixers(x, w["norm_mix_g"], w["w_gate"], w["b_gate"], pf, pb, consts["hgrn_bounds"],
                                  w["hgrn_norm_g"], consts["cos"], consts["sin"], layer,
                                  z_col=2, q_col=3, g_col=4, v_col=1, cq_col=5120 // C_QK_WIDTH,
                                  ck_col=5632 // C_QK_WIDTH, cg_col=6144 // C_V_WIDTH, cv_col=2,
                                  tm=tm_proj, tn=tn_gates)

    ya = _attention(pf, pb, consts["rel_bias"], w["qn_g"], w["kn_g"], layer,
                    q_col=0, k_col=A_WIDTH // A_HEAD_DIM, v_col=0)

    x_new = _merge_out(ya, yb, yc, gates, x, w["w_br_a"], w["w_br_b"], w["w_br_c"], w["w_out"], layer,
                       tm=tm_merge)
    return _mlp(x_new, w["norm_ffn_g"], w["w_up"], w["w_down"], layer, tm=tm_mlp, tf=tf_mlp)


def _block(x, norm_mix_g, w_in, qn_g, kn_g, rel_bias, lb_logits, hgrn_norm_g, w_gate, b_gate,
           w_br_a, w_br_b, w_br_c, w_out, norm_ffn_g, w_up, w_down, **tiles):
    bsz, seq, d = x.shape
    depth = w_in.shape[0]
    row = lambda t: t.astype(F32)[:, None, :]
    w = dict(norm_mix_g=row(norm_mix_g), w_in=w_in.astype(BF16), qn_g=row(qn_g), kn_g=row(kn_g),
             hgrn_norm_g=row(hgrn_norm_g), w_gate=w_gate.astype(BF16), b_gate=row(b_gate),
             w_br_a=w_br_a.astype(BF16), w_br_b=w_br_b.astype(BF16), w_br_c=w_br_c.astype(BF16),
             w_out=w_out.astype(BF16), norm_ffn_g=row(norm_ffn_g), w_up=w_up.astype(BF16),
             w_down=w_down.astype(BF16))
    cos, sin = _rotary_tables(seq)
    consts = dict(rel_bias=_rel_bias_window(rel_bias), hgrn_bounds=_hgrn_bound_consts(lb_logits),
                  cos=cos, sin=sin)
    rows = x.reshape(bsz * seq, d)
    outs = []
    for b in range(bsz):
        xb = rows[b * seq:(b + 1) * seq]
        for layer in range(depth):
            xb = _layer(xb, layer, w, consts, **tiles)
        outs.append(xb)
    return jnp.concatenate(outs, axis=0).reshape(bsz, seq, d)


def kernel(x, norm_mix_g, w_in, qn_g, kn_g, rel_bias, lb_logits, hgrn_norm_g, w_gate, b_gate,
           w_br_a, w_br_b, w_br_c, w_out, norm_ffn_g, w_up, w_down):
    return _block(x, norm_mix_g, w_in, qn_g, kn_g, rel_bias, lb_logits, hgrn_norm_g, w_gate, b_gate,
                  w_br_a, w_br_b, w_br_c, w_out, norm_ffn_g, w_up, w_down,
                  tm_proj=1024, tn_gates=768, tm_merge=256, tm_mlp=1024, tf_mlp=1024)
```

```python
import functools

import numpy as np
import jax
import jax.numpy as jnp
from jax import lax
from jax.experimental import pallas as pl
from jax.experimental.pallas import tpu as pltpu

D_MODEL = 2048
CHUNK = 64
EPS = 1e-6

A_HEADS = 8
A_HEAD_DIM = 128
A_WIDTH = A_HEADS * A_HEAD_DIM
A_PAST_CHUNKS = 8
REL_CLIP = 256

B_HEADS = 8
B_KEY_DIM = 128
B_VAL_DIM = 128
B_WIDTH = B_HEADS * B_KEY_DIM

C_HEADS = 8
C_QK_DIM = 64
C_V_DIM = 128
C_QK_WIDTH = C_HEADS * C_QK_DIM
C_V_WIDTH = C_HEADS * C_V_DIM
ROPE_BASE = 10000.0

N_BRANCH = 3

LANES = 128
VMEM_LIMIT = 60 * 1024 * 1024

BF16 = jnp.bfloat16
F32 = jnp.float32
NT_DIMS = (((1,), (1,)), ((), ()))
TN_DIMS = (((0,), (0,)), ((), ()))
LOG2E = float(np.log2(np.e))


def _params(*semantics):
    return pltpu.CompilerParams(dimension_semantics=semantics, vmem_limit_bytes=VMEM_LIMIT)


def _rms(t, gain=None):
    y = t * lax.rsqrt(jnp.mean(t * t, axis=-1, keepdims=True) + EPS)
    return y if gain is None else y * gain


def _silu(t):
    return t / (1.0 + jnp.exp(-t))


def _neg_abs(t):
    bits = lax.bitcast_convert_type(t, jnp.uint32) | jnp.uint32(0x80000000)
    return lax.bitcast_convert_type(bits, F32)


PROJ_TILE = 1024
F32_TILES = (3, 4, 6, 7, 9)
BF16_TILES = (0, 1, 2, 5, 8)
A_Q_SCALE = A_HEAD_DIM ** -0.5 * LOG2E


def _proj_tile_of(j):
    n = len(F32_TILES)
    k = j - n
    f32_tile = 3 + j + (j >= 2).astype(jnp.int32) + (j >= 4).astype(jnp.int32)
    bf16_tile = k + 2 * (k >= 3).astype(jnp.int32) + 2 * (k >= 4).astype(jnp.int32)
    return jnp.where(j < n, f32_tile, bf16_tile)


def _proj_kernel(x_ref, g_ref, w_ref, gq_ref, gk_ref, pf_ref, pb_ref, h_ref):
    j = pl.program_id(1)
    n_f32 = len(F32_TILES)

    @pl.when(j == 0)
    def _():
        h_ref[...] = _rms(x_ref[...], g_ref[...]).astype(BF16)

    def product():
        return jnp.dot(h_ref[...], w_ref[...], preferred_element_type=F32)

    def head_norm(gain_ref, scale):
        acc = product()
        for h in range(A_HEADS):
            cols = slice(h * A_HEAD_DIM, (h + 1) * A_HEAD_DIM)
            pb_ref[:, cols] = (_rms(acc[:, cols], gain_ref[...]) * scale).astype(BF16)

    @pl.when(j < n_f32)
    def _():
        pf_ref[...] = product()

    pl.when(j == n_f32)(functools.partial(head_norm, gq_ref, A_Q_SCALE))
    pl.when(j == n_f32 + 1)(functools.partial(head_norm, gk_ref, 1.0))

    @pl.when(j > n_f32 + 1)
    def _():
        pb_ref[...] = product().astype(BF16)


def _proj(x, g, w, gq, gk, layer, *, tm):
    s, d = x.shape
    n_f32, n_bf16 = len(F32_TILES), len(BF16_TILES)
    assert BF16_TILES[:2] == (0, 1) and PROJ_TILE == A_WIDTH
    gain_spec = pl.BlockSpec((None, 1, A_HEAD_DIM), lambda i, j: (layer, 0, 0))
    return pl.pallas_call(
        _proj_kernel,
        grid=(s // tm, n_f32 + n_bf16),
        in_specs=[
            pl.BlockSpec((tm, d), lambda i, j: (i, 0)),
            pl.BlockSpec((None, 1, d), lambda i, j: (layer, 0, 0)),
            pl.BlockSpec((None, d, PROJ_TILE), lambda i, j: (layer, 0, _proj_tile_of(j))),
            gain_spec, gain_spec,
        ],
        out_specs=[pl.BlockSpec((tm, PROJ_TILE), lambda i, j: (i, jnp.minimum(j, n_f32 - 1))),
                   pl.BlockSpec((tm, PROJ_TILE), lambda i, j: (i, jnp.maximum(j - n_f32, 0))),
                   pl.BlockSpec((tm, d), lambda i, j: (i, 0))],
        out_shape=[jax.ShapeDtypeStruct((s, n_f32 * PROJ_TILE), F32),
                   jax.ShapeDtypeStruct((s, n_bf16 * PROJ_TILE), BF16),
                   jax.ShapeDtypeStruct((s, d), BF16)],
        compiler_params=_params("arbitrary", "arbitrary"),
        name="proj",
    )(x, g, w, gq, gk)


A_GROUP = A_PAST_CHUNKS * CHUNK
A_SUB = 4 * CHUNK
A_SUB_WINDOW = A_SUB + A_PAST_CHUNKS * CHUNK
A_STEP = 4 * A_GROUP


def _attn_kernel(q_ref, k_ref, v_ref, bias_ref, o_ref, kbuf, vbuf):
    i = pl.program_id(1)

    @pl.when(i == 0)
    def _():
        kbuf[0:A_GROUP, :] = jnp.zeros((A_GROUP, A_HEAD_DIM), BF16)
        vbuf[0:A_GROUP, :] = jnp.zeros((A_GROUP, A_HEAD_DIM), BF16)

    @pl.when(i > 0)
    def _():
        kbuf[0:A_GROUP, :] = kbuf[A_STEP:, :]
        vbuf[0:A_GROUP, :] = vbuf[A_STEP:, :]

    kbuf[A_GROUP:, :] = k_ref[...]
    vbuf[A_GROUP:, :] = v_ref[...]

    def attend(first_step):
        subs = range(0, A_STEP, A_SUB)
        scores = []
        for lo in subs:
            s = lax.dot_general(q_ref[lo:lo + A_SUB, :], kbuf[lo:lo + A_SUB_WINDOW, :], NT_DIMS,
                                preferred_element_type=F32) + bias_ref[...]
            if first_step and lo < A_GROUP:
                col = lax.broadcasted_iota(jnp.int32, s.shape, 1)
                s = jnp.where(col >= A_GROUP - lo, s, -jnp.inf)
            scores.append(s)
        probs = []
        for s in scores:
            p = jnp.exp2(s - jnp.max(s, axis=-1, keepdims=True))
            probs.append((p.astype(BF16), jnp.sum(p, axis=-1, keepdims=True)))
        for lo, (p, l) in zip(subs, probs):
            o = jnp.dot(p, vbuf[lo:lo + A_SUB_WINDOW, :], preferred_element_type=F32)
            o_ref[lo:lo + A_SUB, :] = (o / l).astype(o_ref.dtype)

    pl.when(i == 0)(functools.partial(attend, True))
    pl.when(i > 0)(functools.partial(attend, False))


def _attention(pb, bias, layer, *, q_col, k_col, v_col):
    s = pb.shape[0]
    blk = (A_STEP, A_HEAD_DIM)
    cur = lambda c: (lambda h, i: (i, c + h))
    return pl.pallas_call(
        _attn_kernel,
        grid=(A_HEADS, s // blk[0]),
        in_specs=[
            pl.BlockSpec(blk, cur(q_col)),
            pl.BlockSpec(blk, cur(k_col)),
            pl.BlockSpec(blk, cur(v_col)),
            pl.BlockSpec((None, None, A_SUB, A_SUB_WINDOW), lambda h, i: (layer, h, 0, 0)),
        ],
        out_specs=pl.BlockSpec(blk, lambda h, i: (i, h)),
        out_shape=jax.ShapeDtypeStruct((s, A_WIDTH), BF16),
        scratch_shapes=[pltpu.VMEM((A_GROUP + A_STEP, A_HEAD_DIM), BF16),
                        pltpu.VMEM((A_GROUP + A_STEP, A_HEAD_DIM), BF16)],
        compiler_params=_params("arbitrary", "arbitrary"),
        name="attention",
    )(pb, pb, pb, bias)


def _rel_bias_window(rel_table):
    t = rel_table.astype(F32)
    lead = t.shape[:-1]
    past = A_PAST_CHUNKS * CHUNK
    n = A_SUB + A_SUB_WINDOW
    n_far = past + A_SUB - 1 - REL_CLIP
    p = jnp.concatenate([jnp.broadcast_to(t[..., 2 * REL_CLIP:], lead + (n_far,)),
                         t[..., ::-1][..., :n - 1 - n_far],
                         jnp.zeros(lead + (1,), F32)], axis=-1)
    v = jnp.roll(p, -(A_SUB - 1), axis=-1)
    rows = jnp.tile(v, A_SUB)[..., :A_SUB * (n - 1)].reshape(lead + (A_SUB, n - 1))
    bias = rows[..., :A_SUB_WINDOW]
    q_chunk = np.arange(A_SUB)[:, None] // CHUNK
    k_chunk = np.arange(A_SUB_WINDOW)[None, :] // CHUNK
    band = (k_chunk >= q_chunk) & (k_chunk <= q_chunk + A_PAST_CHUNKS)
    return jnp.where(band, bias * LOG2E, -jnp.inf)


B_LEVELS = (32, 16, 8, 4, 2, 1)


def _pair_level_matrix():
    t = np.arange(CHUNK)[:, None]
    s = np.arange(CHUNK)[None, :]
    lvl = np.full((CHUNK, CHUNK), -1, np.int32)
    lvl[t == s] = 0
    for n, h in enumerate(B_LEVELS):
        m = (t // (2 * h) == s // (2 * h)) & (t % (2 * h) >= h) & (s % (2 * h) < h)
        lvl[m] = n + 1
    return lvl


def _hgrn_chunk_prep(z, qin, log2lb, log2_1mlb, one_m_lb):
    row = lax.broadcasted_iota(jnp.int32, (CHUNK, B_KEY_DIM), 0)
    z2 = z * LOG2E
    e = jnp.exp2(_neg_abs(z2))
    log2_sig = jnp.minimum(z2, 0.0) - jnp.log2(1.0 + e)
    bb = log2_1mlb + log2_sig
    lf = jnp.maximum(log2lb, bb) + jnp.log2(1.0 + jnp.exp2(_neg_abs(log2lb - bb)))
    kb = (one_m_lb * (jnp.where(z >= 0.0, e, 1.0) / (1.0 + e))).astype(BF16)
    qb = _silu(qin).astype(BF16)

    b = lf
    for sh in (1, 2, 4):
        b = b + jnp.where(row >= sh, pltpu.roll(b, sh, 0), 0.0)
    for sh in (8, 16, 32):
        b = b + jnp.concatenate([jnp.zeros((sh, B_KEY_DIM), F32), b[:CHUNK - sh, :]], axis=0)

    operands = [(qb, kb)]
    for h in B_LEVELS:
        if h >= 4:
            bound = jnp.concatenate(
                [jnp.broadcast_to(b[blk + h - 1:blk + h, :], (2 * h, B_KEY_DIM))
                 for blk in range(0, CHUNK, 2 * h)], axis=0)
            expo = _neg_abs(b - bound)
        elif h == 2:
            r4 = row % 4
            expo = jnp.where(r4 == 0, pltpu.roll(lf, CHUNK - 1, 0),
                             jnp.where(r4 == 1, 0.0,
                                       jnp.where(r4 == 2, lf, lf + pltpu.roll(lf, 1, 0))))
        else:
            expo = jnp.where(row % 2 == 1, lf, 0.0)
        w = jnp.exp2(expo).astype(BF16)
        operands.append((qb * w, kb * w))

    b_last = b[CHUNK - 1:CHUNK, :]
    q_dec = qb * jnp.exp2(b).astype(BF16)
    k_dec = kb * jnp.exp2(b_last - b).astype(BF16)
    return operands, q_dec, k_dec, jnp.exp2(b_last)


def _hgrn_chunk_products(prep, v, level, state_t):
    operands, q_dec, k_dec, chunk_decay = prep
    attn = 0.0
    for n, (qw, kw) in enumerate(operands):
        attn = jnp.where(level == n, lax.dot_general(qw, kw, NT_DIMS, preferred_element_type=F32), attn)
    o_past = lax.dot_general(q_dec, state_t.astype(BF16), NT_DIMS, preferred_element_type=F32)
    new_state_t = chunk_decay * state_t + lax.dot_general(v, k_dec, TN_DIMS, preferred_element_type=F32)
    return attn.astype(BF16), o_past, new_state_t


def _hgrn_chunk_out(attn, o_past, v, g, gain):
    o = o_past + jnp.dot(attn, v, preferred_element_type=F32)
    return _rms(o, gain) * _silu(g)


GATE_PIECE = 256
GATE_ROW_SPLIT = 2


def _gates_mixers_kernel(h_ref, wg_ref, bg_ref, z_ref, q_ref, v_ref, g_ref, log2lb_ref, log2_1mlb_ref,
                         omlb_ref, gain_ref, level_ref, cq_ref, ck_ref, cv_ref, cg_ref, cos_ref, sin_ref,
                         gates_ref, yb_ref, yc_ref, state_ref, ret_state_ref, ret_decay_ref):
    @pl.when(jnp.logical_and(pl.program_id(0) == 0, pl.program_id(1) == 0))
    def _():
        state_ref[...] = jnp.zeros_like(state_ref)
        _retention_init(ret_state_ref, ret_decay_ref)

    level = level_ref[...]
    gain = gain_ref[...]
    head_cols = [slice(h * B_KEY_DIM, (h + 1) * B_KEY_DIM) for h in range(B_HEADS)]
    piece_rows = gates_ref.shape[0] // GATE_ROW_SPLIT

    def gate_piece(p):
        rows = slice((p % GATE_ROW_SPLIT) * piece_rows, (p % GATE_ROW_SPLIT + 1) * piece_rows)
        cols = slice((p // GATE_ROW_SPLIT) * GATE_PIECE, (p // GATE_ROW_SPLIT + 1) * GATE_PIECE)
        acc = jnp.dot(h_ref[rows, :], wg_ref[:, cols], preferred_element_type=F32) + bg_ref[:, cols]
        gates_ref[rows, cols] = jax.nn.sigmoid(acc)

    staged = {}

    def prep(c):
        rows = slice(c * CHUNK, (c + 1) * CHUNK)
        staged[c] = [_hgrn_chunk_prep(z_ref[rows, cols], q_ref[rows, cols], log2lb_ref[:, cols],
                                      log2_1mlb_ref[:, cols], omlb_ref[:, cols]) for cols in head_cols]

    def products(c):
        rows = slice(c * CHUNK, (c + 1) * CHUNK)
        for h, cols in enumerate(head_cols):
            attn, o_past, st = _hgrn_chunk_products(staged[c][h], v_ref[rows, cols], level, state_ref[h])
            state_ref[h] = st
            staged[c][h] = (attn, o_past)

    def out(c):
        rows = slice(c * CHUNK, (c + 1) * CHUNK)
        for (attn, o_past), cols in zip(staged[c], head_cols):
            y = _hgrn_chunk_out(attn, o_past, v_ref[rows, cols], g_ref[rows, cols], gain)
            yb_ref[rows, cols] = y.astype(yb_ref.dtype)

    n_chunks = z_ref.shape[0] // CHUNK
    n_pieces = GATE_ROW_SPLIT * gates_ref.shape[1] // GATE_PIECE
    assert n_chunks == 2 and n_pieces == 6
    prep(0)
    gate_piece(0)
    gate_piece(1)
    products(0)
    prep(1)
    gate_piece(2)
    gate_piece(3)
    ret_scores = _retention_scores(cq_ref, ck_ref, cv_ref, cos_ref, sin_ref, ret_state_ref, ret_decay_ref)
    out(0)
    products(1)
    gate_piece(4)
    _retention_out(ret_scores, cv_ref, cg_ref, yc_ref)
    out(1)
    gate_piece(5)


def _gates_mixers(h, wg, bg, pf, pb, bound_consts, gain, cos, sin, layer, *, z_col, q_col, g_col, v_col,
                  cq_col, ck_col, cg_col, cv_col, tm, tn):
    s, d = h.shape
    n_gate = wg.shape[-1]
    n_col = n_gate // tn
    tb = tm // n_col
    assert tb % CHUNK == 0 and tb * n_col == tm and tn % GATE_PIECE == 0
    level = jnp.asarray(_pair_level_matrix())
    assert B_WIDTH == C_V_WIDTH
    wide = lambda c: pl.BlockSpec((tb, B_WIDTH), lambda i, j: (i * n_col + j, c))
    narrow = lambda c: pl.BlockSpec((tb, C_QK_WIDTH), lambda i, j: (i * n_col + j, c))
    table = pl.BlockSpec((tb, LANES), lambda i, j: (i * n_col + j, 0))
    row = pl.BlockSpec((None, 1, B_WIDTH), lambda i, j: (layer, 0, 0))
    seq_out = pl.BlockSpec((tb, B_WIDTH), lambda i, j: (i * n_col + j, 0))
    return pl.pallas_call(
        _gates_mixers_kernel,
        grid=(s // tm, n_col),
        in_specs=[pl.BlockSpec((tm, d), lambda i, j: (i, 0)),
                  pl.BlockSpec((None, d, tn), lambda i, j: (layer, 0, j)),
                  pl.BlockSpec((None, 1, tn), lambda i, j: (layer, 0, j)),
                  wide(z_col), wide(q_col), wide(v_col), wide(g_col), row, row, row,
                  pl.BlockSpec((None, 1, B_VAL_DIM), lambda i, j: (layer, 0, 0)),
                  pl.BlockSpec((CHUNK, CHUNK), lambda i, j: (0, 0)),
                  narrow(cq_col), narrow(ck_col), wide(cv_col), wide(cg_col), table, table],
        out_specs=[pl.BlockSpec((tm, tn), lambda i, j: (i, j)), seq_out, seq_out],
        out_shape=[jax.ShapeDtypeStruct((s, n_gate), F32), jax.ShapeDtypeStruct((s, B_WIDTH), BF16),
                   jax.ShapeDtypeStruct((s, C_V_WIDTH), BF16)],
        scratch_shapes=[pltpu.VMEM((B_HEADS, B_VAL_DIM, B_KEY_DIM), F32),
                        pltpu.VMEM((C_HEADS, C_QK_DIM, C_V_DIM), F32), pltpu.VMEM((C_HEADS, tb, tb), F32)],
        compiler_params=_params("arbitrary", "arbitrary"),
        name="gates_mixers",
    )(h, wg, bg, pf, pf, pb, pf, *bound_consts, gain, level, pf, pf, pb, pf, cos, sin)


def _hgrn_bound_consts(lb_logits):
    lb_cum = jnp.cumsum(jax.nn.softmax(lb_logits.astype(F32), axis=0), axis=0)
    lb = (lb_cum - lb_cum[0:1])[:, None, :]
    return jnp.log2(lb), jnp.log1p(-lb) * LOG2E, 1.0 - lb


C_LOG_GAMMA = [np.float32(np.log(1.0 - 2.0 ** (-5.0 - h))) for h in range(C_HEADS)]


def _retention_init(state_ref, decay_ref):
    t = decay_ref.shape[-1]
    state_ref[...] = jnp.zeros_like(state_ref)
    ti = lax.broadcasted_iota(jnp.int32, (t, t), 0)
    si = lax.broadcasted_iota(jnp.int32, (t, t), 1)
    rel = (ti - si).astype(F32)
    for h in range(C_HEADS):
        decay_ref[h] = jnp.where(rel >= 0.0, jnp.exp(C_LOG_GAMMA[h] * jnp.maximum(rel, 0.0)), 0.0)


def _retention_scores(q_ref, k_ref, v_ref, cos_ref, sin_ref, state_ref, decay_ref):
    t = q_ref.shape[0]
    cos = cos_ref[...]
    sin = sin_ref[...]
    lane = lax.broadcasted_iota(jnp.int32, (t, LANES), 1)
    first_half = (lane % C_QK_DIM) < (C_QK_DIM // 2)

    def rotary(x):
        swapped = jnp.where(first_half, pltpu.roll(x, LANES - C_QK_DIM // 2, 1),
                            pltpu.roll(x, C_QK_DIM // 2, 1))
        return x * cos + swapped * sin

    pos = lax.broadcasted_iota(jnp.int32, (t, C_QK_DIM), 0).astype(F32)
    partial_out = []
    for pair in range(C_HEADS // 2):
        lanes = slice(pair * LANES, (pair + 1) * LANES)
        q2 = rotary(q_ref[:, lanes])
        k2 = rotary(k_ref[:, lanes]) * (C_QK_DIM ** -0.5)
        for sub in range(2):
            h = 2 * pair + sub
            qh = q2[:, sub * C_QK_DIM:(sub + 1) * C_QK_DIM]
            kh = k2[:, sub * C_QK_DIM:(sub + 1) * C_QK_DIM]
            vh = v_ref[:, h * C_V_DIM:(h + 1) * C_V_DIM]
            scores = lax.dot_general(qh.astype(BF16), kh.astype(BF16), NT_DIMS,
                                     preferred_element_type=F32) * decay_ref[h]
            state = state_ref[h]
            q_dec = qh * jnp.exp(C_LOG_GAMMA[h] * (pos + 1.0))
            o_past = jnp.dot(q_dec.astype(BF16), state.astype(BF16), preferred_element_type=F32)
            k_dec = kh * jnp.exp(C_LOG_GAMMA[h] * (t - 1.0 - pos))
            state_ref[h] = (np.float32(np.exp(C_LOG_GAMMA[h] * t)) * state
                            + lax.dot_general(k_dec.astype(BF16), vh, TN_DIMS, preferred_element_type=F32))
            partial_out.append((scores.astype(BF16), o_past))
    return partial_out


def _retention_out(partial_out, v_ref, g_ref, o_ref):
    for h, (scores, o_past) in enumerate(partial_out):
        cols = slice(h * C_V_DIM, (h + 1) * C_V_DIM)
        o = o_past + jnp.dot(scores, v_ref[:, cols], preferred_element_type=F32)
        o_ref[:, cols] = (_rms(o) * _silu(g_ref[:, cols])).astype(o_ref.dtype)


def _rotary_tables(seq):
    half = C_QK_DIM // 2
    inv_freq = jnp.asarray(1.0 / ROPE_BASE ** np.linspace(0.0, 1.0, half), F32)
    ang = jnp.arange(seq).astype(F32)[:, None] * inv_freq[None, :]
    cos, sin = jnp.cos(ang), jnp.sin(ang)
    reps = LANES // C_QK_DIM
    return (jnp.tile(jnp.concatenate([cos, cos], axis=1), (1, reps)),
            jnp.tile(jnp.concatenate([-sin, sin], axis=1), (1, reps)))


def _merge_kernel(ya_ref, yb_ref, yc_ref, ga_ref, gb_ref, gc_ref, x_ref, wa_ref, wb_ref, wc_ref,
                  wo_ref, xo_ref):
    merged = ga_ref[...] * jnp.dot(ya_ref[...], wa_ref[...], preferred_element_type=F32)
    merged = merged + gb_ref[...] * jnp.dot(yb_ref[...], wb_ref[...], preferred_element_type=F32)
    merged = merged + gc_ref[...] * jnp.dot(yc_ref[...], wc_ref[...], preferred_element_type=F32)
    xo_ref[...] = x_ref[...] + jnp.dot(merged.astype(BF16), wo_ref[...], preferred_element_type=F32)


def _merge_out(ya, yb, yc, gates, x, wa, wb, wc, wo, layer, *, tm):
    s, d = x.shape
    once = pl.Buffered(1)
    y_spec = pl.BlockSpec((tm, A_WIDTH), lambda i: (i, 0))
    gate_spec = lambda c: pl.BlockSpec((tm, d), lambda i: (i, c))
    w_spec = pl.BlockSpec((None, A_WIDTH, d), lambda i: (layer, 0, 0), pipeline_mode=once)
    return pl.pallas_call(
        _merge_kernel,
        grid=(s // tm,),
        in_specs=[y_spec, y_spec, y_spec, gate_spec(0), gate_spec(1), gate_spec(2),
                  pl.BlockSpec((tm, d), lambda i: (i, 0)),
                  w_spec, w_spec, w_spec,
                  pl.BlockSpec((None, d, d), lambda i: (layer, 0, 0), pipeline_mode=once)],
        out_specs=pl.BlockSpec((tm, d), lambda i: (i, 0)),
        out_shape=jax.ShapeDtypeStruct((s, d), F32),
        compiler_params=_params("parallel"),
        name="merge_out",
    )(ya, yb, yc, gates, gates, gates, x, wa, wb, wc, wo)


def _mlp_kernel(x_ref, g_ref, wu_ref, wd_ref, o_ref, h_ref):
    @pl.when(pl.program_id(1) == 0)
    def _():
        x = x_ref[...]
        h_ref[...] = _rms(x, g_ref[...]).astype(BF16)
        o_ref[...] = x

    up = jnp.maximum(jnp.dot(h_ref[...], wu_ref[...], preferred_element_type=F32), 0.0)
    o_ref[...] += jnp.dot((up * up).astype(BF16), wd_ref[...], preferred_element_type=F32)


def _mlp(x, g, wu, wd, layer, *, tm, tf):
    s, d = x.shape
    ff = wu.shape[-1]
    return pl.pallas_call(
        _mlp_kernel,
        grid=(s // tm, ff // tf),
        in_specs=[pl.BlockSpec((tm, d), lambda i, f: (i, 0)),
                  pl.BlockSpec((None, 1, d), lambda i, f: (layer, 0, 0)),
                  pl.BlockSpec((None, d, tf), lambda i, f: (layer, 0, f)),
                  pl.BlockSpec((None, tf, d), lambda i, f: (layer, f, 0))],
        out_specs=pl.BlockSpec((tm, d), lambda i, f: (i, 0)),
        out_shape=jax.ShapeDtypeStruct((s, d), F32),
        scratch_shapes=[pltpu.VMEM((tm, d), BF16)],
        compiler_params=_params("parallel", "arbitrary"),
        name="mlp",
    )(x, g, wu, wd)


def _layer(x, layer, w, consts, *, tm_proj, tn_gates, tm_merge, tm_mlp, tf_mlp):
    pf, pb, h = _proj(x, w["norm_mix_g"], w["w_in"], w["qn_g"], w["kn_g"], layer, tm=tm_proj)
    gates, yb, yc = _gates_mixers(h, w["w_gate"], w["b_gate"], pf, pb, consts["hgrn_bounds"],
                                  w["hgrn_norm_g"], consts["cos"], consts["sin"], layer,
                                  z_col=0, q_col=1, g_col=2, v_col=3, cq_col=3072 // C_QK_WIDTH,
                                  ck_col=3584 // C_QK_WIDTH, cg_col=4096 // C_V_WIDTH, cv_col=4,
                                  tm=tm_proj, tn=tn_gates)
    heads_per_tile = PROJ_TILE // A_HEAD_DIM
    ya = _attention(pb, consts["rel_bias"], layer, q_col=0, k_col=heads_per_tile, v_col=2 * heads_per_tile)

    x_new = _merge_out(ya, yb, yc, gates, x, w["w_br_a"], w["w_br_b"], w["w_br_c"], w["w_out"], layer,
                       tm=tm_merge)
    return _mlp(x_new, w["norm_ffn_g"], w["w_up"], w["w_down"], layer, tm=tm_mlp, tf=tf_mlp)


def _block(x, norm_mix_g, w_in, qn_g, kn_g, rel_bias, lb_logits, hgrn_norm_g, w_gate, b_gate,
           w_br_a, w_br_b, w_br_c, w_out, norm_ffn_g, w_up, w_down, **tiles):
    bsz, seq, d = x.shape
    depth = w_in.shape[0]
    row = lambda t: t.astype(F32)[:, None, :]
    w = dict(norm_mix_g=row(norm_mix_g), w_in=w_in.astype(BF16), qn_g=row(qn_g), kn_g=row(kn_g),
             hgrn_norm_g=row(hgrn_norm_g), w_gate=w_gate.astype(BF16), b_gate=row(b_gate),
             w_br_a=w_br_a.astype(BF16), w_br_b=w_br_b.astype(BF16), w_br_c=w_br_c.astype(BF16),
             w_out=w_out.astype(BF16), norm_ffn_g=row(norm_ffn_g), w_up=w_up.astype(BF16),
             w_down=w_down.astype(BF16))
    cos, sin = _rotary_tables(seq)
    consts = dict(rel_bias=_rel_bias_window(rel_bias), hgrn_bounds=_hgrn_bound_consts(lb_logits),
                  cos=cos, sin=sin)
    rows = x.reshape(bsz * seq, d)
    outs = []
    for b in range(bsz):
        xb = rows[b * seq:(b + 1) * seq]
        for layer in range(depth):
            xb = _layer(xb, layer, w, consts, **tiles)
        outs.append(xb)
    return jnp.concatenate(outs, axis=0).reshape(bsz, seq, d)


def kernel(x, norm_mix_g, w_in, qn_g, kn_g, rel_bias, lb_logits, hgrn_norm_g, w_gate, b_gate,
           w_br_a, w_br_b, w_br_c, w_out, norm_ffn_g, w_up, w_down):
    return _block(x, norm_mix_g, w_in, qn_g, kn_g, rel_bias, lb_logits, hgrn_norm_g, w_gate, b_gate,
                  w_br_a, w_br_b, w_br_c, w_out, norm_ffn_g, w_up, w_down,
                  tm_proj=1024, tn_gates=768, tm_merge=256, tm_mlp=1024, tf_mlp=1024)
```

```python
import functools

import numpy as np
import jax
import jax.numpy as jnp
from jax import lax
from jax.experimental import pallas as pl
from jax.experimental.pallas import tpu as pltpu

D_MODEL = 2048
CHUNK = 64
EPS = 1e-6

A_HEADS = 8
A_HEAD_DIM = 128
A_WIDTH = A_HEADS * A_HEAD_DIM
A_PAST_CHUNKS = 8
REL_CLIP = 256

B_HEADS = 8
B_KEY_DIM = 128
B_VAL_DIM = 128
B_WIDTH = B_HEADS * B_KEY_DIM

C_HEADS = 8
C_QK_DIM = 64
C_V_DIM = 128
C_QK_WIDTH = C_HEADS * C_QK_DIM
C_V_WIDTH = C_HEADS * C_V_DIM
ROPE_BASE = 10000.0

N_BRANCH = 3

LANES = 128
VMEM_LIMIT = 60 * 1024 * 1024

BF16 = jnp.bfloat16
F32 = jnp.float32
NT_DIMS = (((1,), (1,)), ((), ()))
TN_DIMS = (((0,), (0,)), ((), ()))
LOG2E = float(np.log2(np.e))


def _params(*semantics):
    return pltpu.CompilerParams(dimension_semantics=semantics, vmem_limit_bytes=VMEM_LIMIT)


def _rms(t, gain=None):
    y = t * lax.rsqrt(jnp.mean(t * t, axis=-1, keepdims=True) + EPS)
    return y if gain is None else y * gain


def _silu(t):
    return t / (1.0 + jnp.exp(-t))


def _neg_abs(t):
    bits = lax.bitcast_convert_type(t, jnp.uint32) | jnp.uint32(0x80000000)
    return lax.bitcast_convert_type(bits, F32)


PROJ_TILE = 1024
F32_TILES = (3, 4, 6, 7, 9)
BF16_TILES = (0, 1, 2, 5, 8)
A_Q_SCALE = A_HEAD_DIM ** -0.5 * LOG2E


def _proj_tile_of(j):
    n = len(F32_TILES)
    k = j - n
    f32_tile = 3 + j + (j >= 2).astype(jnp.int32) + (j >= 4).astype(jnp.int32)
    bf16_tile = k + 2 * (k >= 3).astype(jnp.int32) + 2 * (k >= 4).astype(jnp.int32)
    return jnp.where(j < n, f32_tile, bf16_tile)


def _proj_kernel(x_ref, g_ref, w_ref, gq_ref, gk_ref, pf_ref, pb_ref, h_ref):
    j = pl.program_id(1)
    n_f32 = len(F32_TILES)

    @pl.when(j == 0)
    def _():
        h_ref[...] = _rms(x_ref[...], g_ref[...]).astype(BF16)

    def product():
        return jnp.dot(h_ref[...], w_ref[...], preferred_element_type=F32)

    def head_norm(gain_ref, scale):
        acc = product()
        for h in range(A_HEADS):
            cols = slice(h * A_HEAD_DIM, (h + 1) * A_HEAD_DIM)
            pb_ref[:, cols] = (_rms(acc[:, cols], gain_ref[...]) * scale).astype(BF16)

    @pl.when(j < n_f32)
    def _():
        pf_ref[...] = product()

    pl.when(j == n_f32)(functools.partial(head_norm, gq_ref, A_Q_SCALE))
    pl.when(j == n_f32 + 1)(functools.partial(head_norm, gk_ref, 1.0))

    @pl.when(j > n_f32 + 1)
    def _():
        pb_ref[...] = product().astype(BF16)


def _proj(x, g, w, gq, gk, layer, *, tm):
    s, d = x.shape
    n_f32, n_bf16 = len(F32_TILES), len(BF16_TILES)
    assert BF16_TILES[:2] == (0, 1) and PROJ_TILE == A_WIDTH and s % tm == 0
    gain_spec = pl.BlockSpec((None, 1, A_HEAD_DIM), lambda i, j: (layer, 0, 0))
    return pl.pallas_call(
        _proj_kernel,
        grid=(s // tm, n_f32 + n_bf16),
        in_specs=[
            pl.BlockSpec((tm, d), lambda i, j: (i, 0)),
            pl.BlockSpec((None, 1, d), lambda i, j: (layer, 0, 0)),
            pl.BlockSpec((None, d, PROJ_TILE), lambda i, j: (layer, 0, _proj_tile_of(j))),
            gain_spec, gain_spec,
        ],
        out_specs=[pl.BlockSpec((tm, PROJ_TILE), lambda i, j: (i, jnp.minimum(j, n_f32 - 1))),
                   pl.BlockSpec((tm, PROJ_TILE), lambda i, j: (i, jnp.maximum(j - n_f32, 0))),
                   pl.BlockSpec((tm, d), lambda i, j: (i, 0))],
        out_shape=[jax.ShapeDtypeStruct((s, n_f32 * PROJ_TILE), F32),
                   jax.ShapeDtypeStruct((s, n_bf16 * PROJ_TILE), BF16),
                   jax.ShapeDtypeStruct((s, d), BF16)],
        compiler_params=_params("arbitrary", "arbitrary"),
        name="proj",
    )(x, g, w, gq, gk)


A_GROUP = A_PAST_CHUNKS * CHUNK
A_SUB = 4 * CHUNK
A_SUB_WINDOW = A_SUB + A_PAST_CHUNKS * CHUNK
A_STEP = 8 * A_GROUP


def _attn_kernel(q_ref, k_ref, v_ref, bias_ref, o_ref, kbuf, vbuf):
    i = pl.program_id(1)

    @pl.when(i == 0)
    def _():
        kbuf[0:A_GROUP, :] = jnp.zeros((A_GROUP, A_HEAD_DIM), BF16)
        vbuf[0:A_GROUP, :] = jnp.zeros((A_GROUP, A_HEAD_DIM), BF16)

    @pl.when(i > 0)
    def _():
        kbuf[0:A_GROUP, :] = kbuf[A_STEP:, :]
        vbuf[0:A_GROUP, :] = vbuf[A_STEP:, :]

    kbuf[A_GROUP:, :] = k_ref[...]
    vbuf[A_GROUP:, :] = v_ref[...]

    def attend(first_step):
        subs = range(0, A_STEP, A_SUB)
        scores = []
        for lo in subs:
            s = lax.dot_general(q_ref[lo:lo + A_SUB, :], kbuf[lo:lo + A_SUB_WINDOW, :], NT_DIMS,
                                preferred_element_type=F32) + bias_ref[...]
            if first_step and lo < A_GROUP:
                col = lax.broadcasted_iota(jnp.int32, s.shape, 1)
                s = jnp.where(col >= A_GROUP - lo, s, -jnp.inf)
            scores.append(s)
        probs = []
        for s in scores:
            p = jnp.exp2(s - jnp.max(s, axis=-1, keepdims=True))
            probs.append((p.astype(BF16), jnp.sum(p, axis=-1, keepdims=True)))
        for lo, (p, l) in zip(subs, probs):
            o = jnp.dot(p, vbuf[lo:lo + A_SUB_WINDOW, :], preferred_element_type=F32)
            o_ref[lo:lo + A_SUB, :] = (o / l).astype(o_ref.dtype)

    pl.when(i == 0)(functools.partial(attend, True))
    pl.when(i > 0)(functools.partial(attend, False))


def _attention(pb, bias, layer, *, q_col, k_col, v_col):
    s = pb.shape[0]
    assert s % A_STEP == 0
    blk = (A_STEP, A_HEAD_DIM)
    cur = lambda c: (lambda h, i: (i, c + h))
    return pl.pallas_call(
        _attn_kernel,
        grid=(A_HEADS, s // blk[0]),
        in_specs=[
            pl.BlockSpec(blk, cur(q_col)),
            pl.BlockSpec(blk, cur(k_col)),
            pl.BlockSpec(blk, cur(v_col)),
            pl.BlockSpec((None, None, A_SUB, A_SUB_WINDOW), lambda h, i: (layer, h, 0, 0)),
        ],
        out_specs=pl.BlockSpec(blk, lambda h, i: (i, h)),
        out_shape=jax.ShapeDtypeStruct((s, A_WIDTH), BF16),
        scratch_shapes=[pltpu.VMEM((A_GROUP + A_STEP, A_HEAD_DIM), BF16),
                        pltpu.VMEM((A_GROUP + A_STEP, A_HEAD_DIM), BF16)],
        compiler_params=_params("arbitrary", "arbitrary"),
        name="attention",
    )(pb, pb, pb, bias)


def _rel_bias_window(rel_table):
    t = rel_table.astype(F32)
    lead = t.shape[:-1]
    past = A_PAST_CHUNKS * CHUNK
    n = A_SUB + A_SUB_WINDOW
    n_far = past + A_SUB - 1 - REL_CLIP
    p = jnp.concatenate([jnp.broadcast_to(t[..., 2 * REL_CLIP:], lead + (n_far,)),
                         t[..., ::-1][..., :n - 1 - n_far],
                         jnp.zeros(lead + (1,), F32)], axis=-1)
    v = jnp.roll(p, -(A_SUB - 1), axis=-1)
    rows = jnp.tile(v, A_SUB)[..., :A_SUB * (n - 1)].reshape(lead + (A_SUB, n - 1))
    bias = rows[..., :A_SUB_WINDOW]
    q_chunk = np.arange(A_SUB)[:, None] // CHUNK
    k_chunk = np.arange(A_SUB_WINDOW)[None, :] // CHUNK
    band = (k_chunk >= q_chunk) & (k_chunk <= q_chunk + A_PAST_CHUNKS)
    return jnp.where(band, bias * LOG2E, -jnp.inf)


B_LEVELS = (32, 16, 8, 4, 2, 1)


def _pair_level_matrix():
    t = np.arange(CHUNK)[:, None]
    s = np.arange(CHUNK)[None, :]
    lvl = np.full((CHUNK, CHUNK), -1, np.int32)
    lvl[t == s] = 0
    for n, h in enumerate(B_LEVELS):
        m = (t // (2 * h) == s // (2 * h)) & (t % (2 * h) >= h) & (s % (2 * h) < h)
        lvl[m] = n + 1
    return lvl


def _hgrn_chunk_prep(z, qin, log2lb, log2_1mlb, one_m_lb):
    row = lax.broadcasted_iota(jnp.int32, (CHUNK, B_KEY_DIM), 0)
    z2 = z * LOG2E
    e = jnp.exp2(_neg_abs(z2))
    log2_sig = jnp.minimum(z2, 0.0) - jnp.log2(1.0 + e)
    bb = log2_1mlb + log2_sig
    lf = jnp.maximum(log2lb, bb) + jnp.log2(1.0 + jnp.exp2(_neg_abs(log2lb - bb)))
    kb = (one_m_lb * (jnp.where(z >= 0.0, e, 1.0) / (1.0 + e))).astype(BF16)
    qb = _silu(qin).astype(BF16)

    b = lf
    for sh in (1, 2, 4):
        b = b + jnp.where(row >= sh, pltpu.roll(b, sh, 0), 0.0)
    for sh in (8, 16, 32):
        b = b + jnp.concatenate([jnp.zeros((sh, B_KEY_DIM), F32), b[:CHUNK - sh, :]], axis=0)

    operands = [(qb, kb)]
    for h in B_LEVELS:
        if h >= 4:
            bound = jnp.concatenate(
                [jnp.broadcast_to(b[blk + h - 1:blk + h, :], (2 * h, B_KEY_DIM))
                 for blk in range(0, CHUNK, 2 * h)], axis=0)
            expo = _neg_abs(b - bound)
        elif h == 2:
            r4 = row % 4
            expo = jnp.where(r4 == 0, pltpu.roll(lf, CHUNK - 1, 0),
                             jnp.where(r4 == 1, 0.0,
                                       jnp.where(r4 == 2, lf, lf + pltpu.roll(lf, 1, 0))))
        else:
            expo = jnp.where(row % 2 == 1, lf, 0.0)
        w = jnp.exp2(expo).astype(BF16)
        operands.append((qb * w, kb * w))

    b_last = b[CHUNK - 1:CHUNK, :]
    q_dec = qb * jnp.exp2(b).astype(BF16)
    k_dec = kb * jnp.exp2(b_last - b).astype(BF16)
    return operands, q_dec, k_dec, jnp.exp2(b_last)


def _hgrn_chunk_products(prep, v, level, state_t):
    operands, q_dec, k_dec, chunk_decay = prep
    attn = 0.0
    for n, (qw, kw) in enumerate(operands):
        attn = jnp.where(level == n, lax.dot_general(qw, kw, NT_DIMS, preferred_element_type=F32), attn)
    o_past = lax.dot_general(q_dec, state_t.astype(BF16), NT_DIMS, preferred_element_type=F32)
    new_state_t = chunk_decay * state_t + lax.dot_general(v, k_dec, TN_DIMS, preferred_element_type=F32)
    return attn.astype(BF16), o_past, new_state_t


def _hgrn_chunk_out(attn, o_past, v, g, gain):
    o = o_past + jnp.dot(attn, v, preferred_element_type=F32)
    return _rms(o, gain) * _silu(g)


GATE_PIECE = 256
GATE_ROW_SPLIT = 2


def _gates_mixers_kernel(h_ref, wg_ref, bg_ref, z_ref, q_ref, v_ref, g_ref, log2lb_ref, log2_1mlb_ref,
                         omlb_ref, gain_ref, level_ref, cq_ref, ck_ref, cv_ref, cg_ref, cos_ref, sin_ref,
                         gates_ref, yb_ref, yc_ref, state_ref, ret_state_ref, ret_decay_ref):
    @pl.when(jnp.logical_and(pl.program_id(0) == 0, pl.program_id(1) == 0))
    def _():
        state_ref[...] = jnp.zeros_like(state_ref)
        _retention_init(ret_state_ref, ret_decay_ref)

    level = level_ref[...]
    gain = gain_ref[...]
    head_cols = [slice(h * B_KEY_DIM, (h + 1) * B_KEY_DIM) for h in range(B_HEADS)]
    piece_rows = gates_ref.shape[0] // GATE_ROW_SPLIT

    def gate_piece(p):
        rows = slice((p % GATE_ROW_SPLIT) * piece_rows, (p % GATE_ROW_SPLIT + 1) * piece_rows)
        cols = slice((p // GATE_ROW_SPLIT) * GATE_PIECE, (p // GATE_ROW_SPLIT + 1) * GATE_PIECE)
        acc = jnp.dot(h_ref[rows, :], wg_ref[:, cols], preferred_element_type=F32) + bg_ref[:, cols]
        gates_ref[rows, cols] = jax.nn.sigmoid(acc)

    staged = {}

    def prep(c):
        rows = slice(c * CHUNK, (c + 1) * CHUNK)
        staged[c] = [_hgrn_chunk_prep(z_ref[rows, cols], q_ref[rows, cols], log2lb_ref[:, cols],
                                      log2_1mlb_ref[:, cols], omlb_ref[:, cols]) for cols in head_cols]

    def products(c):
        rows = slice(c * CHUNK, (c + 1) * CHUNK)
        for h, cols in enumerate(head_cols):
            attn, o_past, st = _hgrn_chunk_products(staged[c][h], v_ref[rows, cols], level, state_ref[h])
            state_ref[h] = st
            staged[c][h] = (attn, o_past)

    def out(c):
        rows = slice(c * CHUNK, (c + 1) * CHUNK)
        for (attn, o_past), cols in zip(staged[c], head_cols):
            y = _hgrn_chunk_out(attn, o_past, v_ref[rows, cols], g_ref[rows, cols], gain)
            yb_ref[rows, cols] = y.astype(yb_ref.dtype)

    n_chunks = z_ref.shape[0] // CHUNK
    n_pieces = GATE_ROW_SPLIT * gates_ref.shape[1] // GATE_PIECE
    assert n_chunks == 2 and n_pieces == 6
    prep(0)
    gate_piece(0)
    gate_piece(1)
    products(0)
    prep(1)
    gate_piece(2)
    gate_piece(3)
    ret_scores = _retention_scores(cq_ref, ck_ref, cv_ref, cos_ref, sin_ref, ret_state_ref, ret_decay_ref)
    out(0)
    products(1)
    gate_piece(4)
    _retention_out(ret_scores, cv_ref, cg_ref, yc_ref)
    out(1)
    gate_piece(5)


def _gates_mixers(h, wg, bg, pf, pb, bound_consts, gain, cos, sin, layer, *, z_col, q_col, g_col, v_col,
                  cq_col, ck_col, cg_col, cv_col, tm, tn):
    s, d = h.shape
    n_gate = wg.shape[-1]
    n_col = n_gate // tn
    tb = tm // n_col
    assert tb % CHUNK == 0 and tb * n_col == tm and tn % GATE_PIECE == 0 and s % tm == 0
    level = jnp.asarray(_pair_level_matrix())
    assert B_WIDTH == C_V_WIDTH
    wide = lambda c: pl.BlockSpec((tb, B_WIDTH), lambda i, j: (i * n_col + j, c))
    narrow = lambda c: pl.BlockSpec((tb, C_QK_WIDTH), lambda i, j: (i * n_col + j, c))
    table = pl.BlockSpec((tb, LANES), lambda i, j: (i * n_col + j, 0))
    row = pl.BlockSpec((None, 1, B_WIDTH), lambda i, j: (layer, 0, 0))
    seq_out = pl.BlockSpec((tb, B_WIDTH), lambda i, j: (i * n_col + j, 0))
    return pl.pallas_call(
        _gates_mixers_kernel,
        grid=(s // tm, n_col),
        in_specs=[pl.BlockSpec((tm, d), lambda i, j: (i, 0)),
                  pl.BlockSpec((None, d, tn), lambda i, j: (layer, 0, j)),
                  pl.BlockSpec((None, 1, tn), lambda i, j: (layer, 0, j)),
                  wide(z_col), wide(q_col), wide(v_col), wide(g_col), row, row, row,
                  pl.BlockSpec((None, 1, B_VAL_DIM), lambda i, j: (layer, 0, 0)),
                  pl.BlockSpec((CHUNK, CHUNK), lambda i, j: (0, 0)),
                  narrow(cq_col), narrow(ck_col), wide(cv_col), wide(cg_col), table, table],
        out_specs=[pl.BlockSpec((tm, tn), lambda i, j: (i, j)), seq_out, seq_out],
        out_shape=[jax.ShapeDtypeStruct((s, n_gate), F32), jax.ShapeDtypeStruct((s, B_WIDTH), BF16),
                   jax.ShapeDtypeStruct((s, C_V_WIDTH), BF16)],
        scratch_shapes=[pltpu.VMEM((B_HEADS, B_VAL_DIM, B_KEY_DIM), F32),
                        pltpu.VMEM((C_HEADS, C_QK_DIM, C_V_DIM), F32), pltpu.VMEM((C_HEADS, tb, tb), F32)],
        compiler_params=_params("arbitrary", "arbitrary"),
        name="gates_mixers",
    )(h, wg, bg, pf, pf, pb, pf, *bound_consts, gain, level, pf, pf, pb, pf, cos, sin)


def _hgrn_bound_consts(lb_logits):
    lb_cum = jnp.cumsum(jax.nn.softmax(lb_logits.astype(F32), axis=0), axis=0)
    lb = (lb_cum - lb_cum[0:1])[:, None, :]
    return jnp.log2(lb), jnp.log1p(-lb) * LOG2E, 1.0 - lb


C_LOG_GAMMA = [np.float32(np.log(1.0 - 2.0 ** (-5.0 - h))) for h in range(C_HEADS)]


def _retention_init(state_ref, decay_ref):
    t = decay_ref.shape[-1]
    state_ref[...] = jnp.zeros_like(state_ref)
    ti = lax.broadcasted_iota(jnp.int32, (t, t), 0)
    si = lax.broadcasted_iota(jnp.int32, (t, t), 1)
    rel = (ti - si).astype(F32)
    for h in range(C_HEADS):
        decay_ref[h] = jnp.where(rel >= 0.0, jnp.exp(C_LOG_GAMMA[h] * jnp.maximum(rel, 0.0)), 0.0)


def _retention_scores(q_ref, k_ref, v_ref, cos_ref, sin_ref, state_ref, decay_ref):
    t = q_ref.shape[0]
    cos = cos_ref[...]
    sin = sin_ref[...]
    lane = lax.broadcasted_iota(jnp.int32, (t, LANES), 1)
    first_half = (lane % C_QK_DIM) < (C_QK_DIM // 2)

    def rotary(x):
        swapped = jnp.where(first_half, pltpu.roll(x, LANES - C_QK_DIM // 2, 1),
                            pltpu.roll(x, C_QK_DIM // 2, 1))
        return x * cos + swapped * sin

    pos = lax.broadcasted_iota(jnp.int32, (t, C_QK_DIM), 0).astype(F32)
    partial_out = []
    for pair in range(C_HEADS // 2):
        lanes = slice(pair * LANES, (pair + 1) * LANES)
        q2 = rotary(q_ref[:, lanes])
        k2 = rotary(k_ref[:, lanes]) * (C_QK_DIM ** -0.5)
        for sub in range(2):
            h = 2 * pair + sub
            qh = q2[:, sub * C_QK_DIM:(sub + 1) * C_QK_DIM]
            kh = k2[:, sub * C_QK_DIM:(sub + 1) * C_QK_DIM]
            vh = v_ref[:, h * C_V_DIM:(h + 1) * C_V_DIM]
            scores = lax.dot_general(qh.astype(BF16), kh.astype(BF16), NT_DIMS,
                                     preferred_element_type=F32) * decay_ref[h]
            state = state_ref[h]
            q_dec = qh * jnp.exp(C_LOG_GAMMA[h] * (pos + 1.0))
            o_past = jnp.dot(q_dec.astype(BF16), state.astype(BF16), preferred_element_type=F32)
            k_dec = kh * jnp.exp(C_LOG_GAMMA[h] * (t - 1.0 - pos))
            state_ref[h] = (np.float32(np.exp(C_LOG_GAMMA[h] * t)) * state
                            + lax.dot_general(k_dec.astype(BF16), vh, TN_DIMS, preferred_element_type=F32))
            partial_out.append((scores.astype(BF16), o_past))
    return partial_out


def _retention_out(partial_out, v_ref, g_ref, o_ref):
    for h, (scores, o_past) in enumerate(partial_out):
        cols = slice(h * C_V_DIM, (h + 1) * C_V_DIM)
        o = o_past + jnp.dot(scores, v_ref[:, cols], preferred_element_type=F32)
        o_ref[:, cols] = (_rms(o) * _silu(g_ref[:, cols])).astype(o_ref.dtype)


def _rotary_tables(seq):
    half = C_QK_DIM // 2
    inv_freq = jnp.asarray(1.0 / ROPE_BASE ** np.linspace(0.0, 1.0, half), F32)
    ang = jnp.arange(seq).astype(F32)[:, None] * inv_freq[None, :]
    cos, sin = jnp.cos(ang), jnp.sin(ang)
    reps = LANES // C_QK_DIM
    return (jnp.tile(jnp.concatenate([cos, cos], axis=1), (1, reps)),
            jnp.tile(jnp.concatenate([-sin, sin], axis=1), (1, reps)))


def _merge_kernel(ya_ref, yb_ref, yc_ref, ga_ref, gb_ref, gc_ref, x_ref, wa_ref, wb_ref, wc_ref,
                  wo_ref, xo_ref):
    merged = ga_ref[...] * jnp.dot(ya_ref[...], wa_ref[...], preferred_element_type=F32)
    merged = merged + gb_ref[...] * jnp.dot(yb_ref[...], wb_ref[...], preferred_element_type=F32)
    merged = merged + gc_ref[...] * jnp.dot(yc_ref[...], wc_ref[...], preferred_element_type=F32)
    xo_ref[...] = x_ref[...] + jnp.dot(merged.astype(BF16), wo_ref[...], preferred_element_type=F32)


def _merge_out(ya, yb, yc, gates, x, wa, wb, wc, wo, layer, *, tm):
    s, d = x.shape
    assert s % tm == 0
    once = pl.Buffered(1)
    y_spec = pl.BlockSpec((tm, A_WIDTH), lambda i: (i, 0))
    gate_spec = lambda c: pl.BlockSpec((tm, d), lambda i: (i, c))
    w_spec = pl.BlockSpec((None, A_WIDTH, d), lambda i: (layer, 0, 0), pipeline_mode=once)
    return pl.pallas_call(
        _merge_kernel,
        grid=(s // tm,),
        in_specs=[y_spec, y_spec, y_spec, gate_spec(0), gate_spec(1), gate_spec(2),
                  pl.BlockSpec((tm, d), lambda i: (i, 0)),
                  w_spec, w_spec, w_spec,
                  pl.BlockSpec((None, d, d), lambda i: (layer, 0, 0), pipeline_mode=once)],
        out_specs=pl.BlockSpec((tm, d), lambda i: (i, 0)),
        out_shape=jax.ShapeDtypeStruct((s, d), F32),
        compiler_params=_params("parallel"),
        name="merge_out",
    )(ya, yb, yc, gates, gates, gates, x, wa, wb, wc, wo)


def _mlp_kernel(x_ref, g_ref, wu_ref, wd_ref, o_ref, h_ref):
    @pl.when(pl.program_id(1) == 0)
    def _():
        x = x_ref[...]
        h_ref[...] = _rms(x, g_ref[...]).astype(BF16)
        o_ref[...] = x

    up = jnp.maximum(jnp.dot(h_ref[...], wu_ref[...], preferred_element_type=F32), 0.0)
    o_ref[...] += jnp.dot((up * up).astype(BF16), wd_ref[...], preferred_element_type=F32)


def _mlp(x, g, wu, wd, layer, *, tm, tf):
    s, d = x.shape
    ff = wu.shape[-1]
    assert s % tm == 0 and ff % tf == 0
    return pl.pallas_call(
        _mlp_kernel,
        grid=(s // tm, ff // tf),
        in_specs=[pl.BlockSpec((tm, d), lambda i, f: (i, 0)),
                  pl.BlockSpec((None, 1, d), lambda i, f: (layer, 0, 0)),
                  pl.BlockSpec((None, d, tf), lambda i, f: (layer, 0, f)),
                  pl.BlockSpec((None, tf, d), lambda i, f: (layer, f, 0))],
        out_specs=pl.BlockSpec((tm, d), lambda i, f: (i, 0)),
        out_shape=jax.ShapeDtypeStruct((s, d), F32),
        scratch_shapes=[pltpu.VMEM((tm, d), BF16)],
        compiler_params=_params("parallel", "arbitrary"),
        name="mlp",
    )(x, g, wu, wd)


def _layer(x, layer, w, consts, *, tm_proj, tn_gates, tm_merge, tm_mlp, tf_mlp):
    pf, pb, h = _proj(x, w["norm_mix_g"], w["w_in"], w["qn_g"], w["kn_g"], layer, tm=tm_proj)
    gates, yb, yc = _gates_mixers(h, w["w_gate"], w["b_gate"], pf, pb, consts["hgrn_bounds"],
                                  w["hgrn_norm_g"], consts["cos"], consts["sin"], layer,
                                  z_col=0, q_col=1, g_col=2, v_col=3, cq_col=3072 // C_QK_WIDTH,
                                  ck_col=3584 // C_QK_WIDTH, cg_col=4096 // C_V_WIDTH, cv_col=4,
                                  tm=tm_proj, tn=tn_gates)
    heads_per_tile = PROJ_TILE // A_HEAD_DIM
    ya = _attention(pb, consts["rel_bias"], layer, q_col=0, k_col=heads_per_tile, v_col=2 * heads_per_tile)

    x_new = _merge_out(ya, yb, yc, gates, x, w["w_br_a"], w["w_br_b"], w["w_br_c"], w["w_out"], layer,
                       tm=tm_merge)
    return _mlp(x_new, w["norm_ffn_g"], w["w_up"], w["w_down"], layer, tm=tm_mlp, tf=tf_mlp)


def _block(x, norm_mix_g, w_in, qn_g, kn_g, rel_bias, lb_logits, hgrn_norm_g, w_gate, b_gate,
           w_br_a, w_br_b, w_br_c, w_out, norm_ffn_g, w_up, w_down, **tiles):
    bsz, seq, d = x.shape
    depth = w_in.shape[0]
    row = lambda t: t.astype(F32)[:, None, :]
    w = dict(norm_mix_g=row(norm_mix_g), w_in=w_in.astype(BF16), qn_g=row(qn_g), kn_g=row(kn_g),
             hgrn_norm_g=row(hgrn_norm_g), w_gate=w_gate.astype(BF16), b_gate=row(b_gate),
             w_br_a=w_br_a.astype(BF16), w_br_b=w_br_b.astype(BF16), w_br_c=w_br_c.astype(BF16),
             w_out=w_out.astype(BF16), norm_ffn_g=row(norm_ffn_g), w_up=w_up.astype(BF16),
             w_down=w_down.astype(BF16))
    cos, sin = _rotary_tables(seq)
    consts = dict(rel_bias=_rel_bias_window(rel_bias), hgrn_bounds=_hgrn_bound_consts(lb_logits),
                  cos=cos, sin=sin)
    rows = x.reshape(bsz * seq, d)
    outs = []
    for b in range(bsz):
        xb = rows[b * seq:(b + 1) * seq]
        for layer in range(depth):
            xb = _layer(xb, layer, w, consts, **tiles)
        outs.append(xb)
    return jnp.concatenate(outs, axis=0).reshape(bsz, seq, d)


def kernel(x, norm_mix_g, w_in, qn_g, kn_g, rel_bias, lb_logits, hgrn_norm_g, w_gate, b_gate,
           w_br_a, w_br_b, w_br_c, w_out, norm_ffn_g, w_up, w_down):
    return _block(x, norm_mix_g, w_in, qn_g, kn_g, rel_bias, lb_logits, hgrn_norm_g, w_gate, b_gate,
                  w_br_a, w_br_b, w_br_c, w_out, norm_ffn_g, w_up, w_down,
                  tm_proj=1024, tn_gates=768, tm_merge=256, tm_mlp=1024, tf_mlp=1024)
```

```python
import functools

import numpy as np
import jax
import jax.numpy as jnp
from jax import lax
from jax.experimental import pallas as pl
from jax.experimental.pallas import tpu as pltpu

D_MODEL = 2048
CHUNK = 64
EPS = 1e-6

A_HEADS = 8
A_HEAD_DIM = 128
A_WIDTH = A_HEADS * A_HEAD_DIM
A_PAST_CHUNKS = 8
REL_CLIP = 256

B_HEADS = 8
B_KEY_DIM = 128
B_VAL_DIM = 128
B_WIDTH = B_HEADS * B_KEY_DIM

C_HEADS = 8
C_QK_DIM = 64
C_V_DIM = 128
C_QK_WIDTH = C_HEADS * C_QK_DIM
C_V_WIDTH = C_HEADS * C_V_DIM
ROPE_BASE = 10000.0

N_BRANCH = 3

LANES = 128
VMEM_LIMIT = 60 * 1024 * 1024

BF16 = jnp.bfloat16
F32 = jnp.float32
NT_DIMS = (((1,), (1,)), ((), ()))
TN_DIMS = (((0,), (0,)), ((), ()))
LOG2E = float(np.log2(np.e))


def _params(*semantics):
    return pltpu.CompilerParams(dimension_semantics=semantics, vmem_limit_bytes=VMEM_LIMIT)


def _rms(t, gain=None):
    y = t * lax.rsqrt(jnp.mean(t * t, axis=-1, keepdims=True) + EPS)
    return y if gain is None else y * gain


def _silu(t):
    return t / (1.0 + jnp.exp(-t))


PROJ_TILE = 1024
F32_TILES = (3, 4, 6, 7, 9)
BF16_TILES = (0, 1, 2, 5, 8)
A_Q_SCALE = A_HEAD_DIM ** -0.5 * LOG2E


def _proj_tile_of(j):
    n = len(F32_TILES)
    k = j - n
    f32_tile = 3 + j + (j >= 2).astype(jnp.int32) + (j >= 4).astype(jnp.int32)
    bf16_tile = k + 2 * (k >= 3).astype(jnp.int32) + 2 * (k >= 4).astype(jnp.int32)
    return jnp.where(j < n, f32_tile, bf16_tile)


def _proj_kernel(x_ref, g_ref, w_ref, gq_ref, gk_ref, pf_ref, pb_ref, h_ref):
    j = pl.program_id(1)
    n_f32 = len(F32_TILES)

    @pl.when(j == 0)
    def _():
        h_ref[...] = _rms(x_ref[...], g_ref[...]).astype(BF16)

    def product():
        return jnp.dot(h_ref[...], w_ref[...], preferred_element_type=F32)

    def head_norm(gain_ref, scale):
        acc = product()
        for h in range(A_HEADS):
            cols = slice(h * A_HEAD_DIM, (h + 1) * A_HEAD_DIM)
            pb_ref[:, cols] = (_rms(acc[:, cols], gain_ref[...]) * scale).astype(BF16)

    @pl.when(j < n_f32)
    def _():
        pf_ref[...] = product()

    pl.when(j == n_f32)(functools.partial(head_norm, gq_ref, A_Q_SCALE))
    pl.when(j == n_f32 + 1)(functools.partial(head_norm, gk_ref, 1.0))

    @pl.when(j > n_f32 + 1)
    def _():
        pb_ref[...] = product().astype(BF16)


def _proj(x, g, w, gq, gk, layer, *, tm):
    s, d = x.shape
    n_f32, n_bf16 = len(F32_TILES), len(BF16_TILES)
    assert BF16_TILES[:2] == (0, 1) and PROJ_TILE == A_WIDTH and s % tm == 0
    gain_spec = pl.BlockSpec((None, 1, A_HEAD_DIM), lambda i, j: (layer, 0, 0))
    return pl.pallas_call(
        _proj_kernel,
        grid=(s // tm, n_f32 + n_bf16),
        in_specs=[
            pl.BlockSpec((tm, d), lambda i, j: (i, 0)),
            pl.BlockSpec((None, 1, d), lambda i, j: (layer, 0, 0)),
            pl.BlockSpec((None, d, PROJ_TILE), lambda i, j: (layer, 0, _proj_tile_of(j))),
            gain_spec, gain_spec,
        ],
        out_specs=[pl.BlockSpec((tm, PROJ_TILE), lambda i, j: (i, jnp.minimum(j, n_f32 - 1))),
                   pl.BlockSpec((tm, PROJ_TILE), lambda i, j: (i, jnp.maximum(j - n_f32, 0))),
                   pl.BlockSpec((tm, d), lambda i, j: (i, 0))],
        out_shape=[jax.ShapeDtypeStruct((s, n_f32 * PROJ_TILE), F32),
                   jax.ShapeDtypeStruct((s, n_bf16 * PROJ_TILE), BF16),
                   jax.ShapeDtypeStruct((s, d), BF16)],
        compiler_params=_params("arbitrary", "arbitrary"),
        name="proj",
    )(x, g, w, gq, gk)


A_GROUP = A_PAST_CHUNKS * CHUNK
A_SUB = 4 * CHUNK
A_SUB_WINDOW = A_SUB + A_PAST_CHUNKS * CHUNK
A_STEP = 8 * A_GROUP


def _attn_kernel(q_ref, k_ref, v_ref, bias_ref, o_ref, kbuf, vbuf):
    i = pl.program_id(1)

    @pl.when(i == 0)
    def _():
        kbuf[0:A_GROUP, :] = jnp.zeros((A_GROUP, A_HEAD_DIM), BF16)
        vbuf[0:A_GROUP, :] = jnp.zeros((A_GROUP, A_HEAD_DIM), BF16)

    @pl.when(i > 0)
    def _():
        kbuf[0:A_GROUP, :] = kbuf[A_STEP:, :]
        vbuf[0:A_GROUP, :] = vbuf[A_STEP:, :]

    kbuf[A_GROUP:, :] = k_ref[...]
    vbuf[A_GROUP:, :] = v_ref[...]

    def attend(first_step):
        subs = range(0, A_STEP, A_SUB)
        scores = []
        for lo in subs:
            s = lax.dot_general(q_ref[lo:lo + A_SUB, :], kbuf[lo:lo + A_SUB_WINDOW, :], NT_DIMS,
                                preferred_element_type=F32) + bias_ref[...]
            if first_step and lo < A_GROUP:
                col = lax.broadcasted_iota(jnp.int32, s.shape, 1)
                s = jnp.where(col >= A_GROUP - lo, s, -jnp.inf)
            scores.append(s)
        probs = []
        for s in scores:
            p = jnp.exp2(s - jnp.max(s, axis=-1, keepdims=True))
            probs.append((p.astype(BF16), jnp.sum(p, axis=-1, keepdims=True)))
        for lo, (p, l) in zip(subs, probs):
            o = jnp.dot(p, vbuf[lo:lo + A_SUB_WINDOW, :], preferred_element_type=F32)
            o_ref[lo:lo + A_SUB, :] = (o / l).astype(o_ref.dtype)

    pl.when(i == 0)(functools.partial(attend, True))
    pl.when(i > 0)(functools.partial(attend, False))


def _attention(pb, bias, layer, *, q_col, k_col, v_col):
    s = pb.shape[0]
    assert s % A_STEP == 0
    blk = (A_STEP, A_HEAD_DIM)
    cur = lambda c: (lambda h, i: (i, c + h))
    return pl.pallas_call(
        _attn_kernel,
        grid=(A_HEADS, s // blk[0]),
        in_specs=[
            pl.BlockSpec(blk, cur(q_col)),
            pl.BlockSpec(blk, cur(k_col)),
            pl.BlockSpec(blk, cur(v_col)),
            pl.BlockSpec((None, None, A_SUB, A_SUB_WINDOW), lambda h, i: (layer, h, 0, 0)),
        ],
        out_specs=pl.BlockSpec(blk, lambda h, i: (i, h)),
        out_shape=jax.ShapeDtypeStruct((s, A_WIDTH), BF16),
        scratch_shapes=[pltpu.VMEM((A_GROUP + A_STEP, A_HEAD_DIM), BF16),
                        pltpu.VMEM((A_GROUP + A_STEP, A_HEAD_DIM), BF16)],
        compiler_params=_params("arbitrary", "arbitrary"),
        name="attention",
    )(pb, pb, pb, bias)


def _rel_bias_window(rel_table):
    t = rel_table.astype(F32)
    lead = t.shape[:-1]
    past = A_PAST_CHUNKS * CHUNK
    n = A_SUB + A_SUB_WINDOW
    n_far = past + A_SUB - 1 - REL_CLIP
    p = jnp.concatenate([jnp.broadcast_to(t[..., 2 * REL_CLIP:], lead + (n_far,)),
                         t[..., ::-1][..., :n - 1 - n_far],
                         jnp.zeros(lead + (1,), F32)], axis=-1)
    v = jnp.roll(p, -(A_SUB - 1), axis=-1)
    rows = jnp.tile(v, A_SUB)[..., :A_SUB * (n - 1)].reshape(lead + (A_SUB, n - 1))
    bias = rows[..., :A_SUB_WINDOW]
    q_chunk = np.arange(A_SUB)[:, None] // CHUNK
    k_chunk = np.arange(A_SUB_WINDOW)[None, :] // CHUNK
    band = (k_chunk >= q_chunk) & (k_chunk <= q_chunk + A_PAST_CHUNKS)
    return jnp.where(band, bias * LOG2E, -jnp.inf)


B_LEVELS = (32, 16, 8, 4, 2, 1)


def _pair_level_matrix():
    t = np.arange(CHUNK)[:, None]
    s = np.arange(CHUNK)[None, :]
    lvl = np.full((CHUNK, CHUNK), -1, np.int32)
    lvl[t == s] = 0
    for n, h in enumerate(B_LEVELS):
        m = (t // (2 * h) == s // (2 * h)) & (t % (2 * h) >= h) & (s % (2 * h) < h)
        lvl[m] = n + 1
    return lvl


def _hgrn_chunk_prep(z, qin, log2lb, log2_1mlb, one_m_lb):
    row = lax.broadcasted_iota(jnp.int32, (CHUNK, B_KEY_DIM), 0)
    z2 = z * LOG2E
    e = jnp.exp2(-jnp.abs(z2))
    log2_sig = jnp.minimum(z2, 0.0) - jnp.log2(1.0 + e)
    bb = log2_1mlb + log2_sig
    lf = jnp.maximum(log2lb, bb) + jnp.log2(1.0 + jnp.exp2(-jnp.abs(log2lb - bb)))
    kb = (one_m_lb * (jnp.where(z >= 0.0, e, 1.0) / (1.0 + e))).astype(BF16)
    qb = _silu(qin).astype(BF16)

    b = lf
    for sh in (1, 2, 4):
        b = b + jnp.where(row >= sh, pltpu.roll(b, sh, 0), 0.0)
    for sh in (8, 16, 32):
        b = b + jnp.concatenate([jnp.zeros((sh, B_KEY_DIM), F32), b[:CHUNK - sh, :]], axis=0)

    operands = [(qb, kb)]
    for h in B_LEVELS:
        if h >= 8:
            pieces = []
            for blk in range(0, CHUNK, 2 * h):
                bound = jnp.broadcast_to(b[blk + h - 1:blk + h, :], (h, B_KEY_DIM))
                pieces += [bound - b[blk:blk + h, :], b[blk + h:blk + 2 * h, :] - bound]
            expo = jnp.concatenate(pieces, axis=0)
        elif h == 4:
            bound = jnp.concatenate(
                [jnp.broadcast_to(b[blk + h - 1:blk + h, :], (2 * h, B_KEY_DIM))
                 for blk in range(0, CHUNK, 2 * h)], axis=0)
            expo = -jnp.abs(b - bound)
        elif h == 2:
            r4 = row % 4
            expo = jnp.where(r4 == 0, pltpu.roll(lf, CHUNK - 1, 0),
                             jnp.where(r4 == 1, 0.0,
                                       jnp.where(r4 == 2, lf, lf + pltpu.roll(lf, 1, 0))))
        else:
            expo = jnp.where(row % 2 == 1, lf, 0.0)
        w = jnp.exp2(expo).astype(BF16)
        operands.append((qb * w, kb * w))

    b_last = b[CHUNK - 1:CHUNK, :]
    q_dec = qb * jnp.exp2(b).astype(BF16)
    k_dec = kb * jnp.exp2(b_last - b).astype(BF16)
    return operands, q_dec, k_dec, jnp.exp2(b_last)


def _hgrn_chunk_products(prep, v, level, state_t):
    operands, q_dec, k_dec, chunk_decay = prep
    attn = 0.0
    for n, (qw, kw) in enumerate(operands):
        attn = jnp.where(level == n, lax.dot_general(qw, kw, NT_DIMS, preferred_element_type=F32), attn)
    o_past = lax.dot_general(q_dec, state_t.astype(BF16), NT_DIMS, preferred_element_type=F32)
    new_state_t = chunk_decay * state_t + lax.dot_general(v, k_dec, TN_DIMS, preferred_element_type=F32)
    return attn.astype(BF16), o_past, new_state_t


def _hgrn_chunk_out(attn, o_past, v, g, gain):
    o = o_past + jnp.dot(attn, v, preferred_element_type=F32)
    return _rms(o, gain) * _silu(g)


GATE_PIECE = 256
GATE_ROW_SPLIT = 2


def _gates_mixers_kernel(h_ref, wg_ref, bg_ref, z_ref, q_ref, v_ref, g_ref, log2lb_ref, log2_1mlb_ref,
                         omlb_ref, gain_ref, level_ref, cq_ref, ck_ref, cv_ref, cg_ref, cos_ref, sin_ref,
                         gates_ref, yb_ref, yc_ref, state_ref, ret_state_ref, ret_decay_ref):
    @pl.when(jnp.logical_and(pl.program_id(0) == 0, pl.program_id(1) == 0))
    def _():
        state_ref[...] = jnp.zeros_like(state_ref)
        _retention_init(ret_state_ref, ret_decay_ref)

    level = level_ref[...]
    gain = gain_ref[...]
    head_cols = [slice(h * B_KEY_DIM, (h + 1) * B_KEY_DIM) for h in range(B_HEADS)]
    piece_rows = gates_ref.shape[0] // GATE_ROW_SPLIT

    def gate_piece(p):
        rows = slice((p % GATE_ROW_SPLIT) * piece_rows, (p % GATE_ROW_SPLIT + 1) * piece_rows)
        cols = slice((p // GATE_ROW_SPLIT) * GATE_PIECE, (p // GATE_ROW_SPLIT + 1) * GATE_PIECE)
        acc = jnp.dot(h_ref[rows, :], wg_ref[:, cols], preferred_element_type=F32) + bg_ref[:, cols]
        gates_ref[rows, cols] = jax.nn.sigmoid(acc)

    staged = {}

    def prep(c):
        rows = slice(c * CHUNK, (c + 1) * CHUNK)
        staged[c] = [_hgrn_chunk_prep(z_ref[rows, cols], q_ref[rows, cols], log2lb_ref[:, cols],
                                      log2_1mlb_ref[:, cols], omlb_ref[:, cols]) for cols in head_cols]

    def products(c):
        rows = slice(c * CHUNK, (c + 1) * CHUNK)
        for h, cols in enumerate(head_cols):
            attn, o_past, st = _hgrn_chunk_products(staged[c][h], v_ref[rows, cols], level, state_ref[h])
            state_ref[h] = st
            staged[c][h] = (attn, o_past)

    def out(c):
        rows = slice(c * CHUNK, (c + 1) * CHUNK)
        for (attn, o_past), cols in zip(staged[c], head_cols):
            y = _hgrn_chunk_out(attn, o_past, v_ref[rows, cols], g_ref[rows, cols], gain)
            yb_ref[rows, cols] = y.astype(yb_ref.dtype)

    n_chunks = z_ref.shape[0] // CHUNK
    n_pieces = GATE_ROW_SPLIT * gates_ref.shape[1] // GATE_PIECE
    assert n_chunks == 2 and n_pieces == 6
    prep(0)
    gate_piece(0)
    gate_piece(1)
    products(0)
    prep(1)
    gate_piece(2)
    gate_piece(3)
    ret_scores = _retention_scores(cq_ref, ck_ref, cv_ref, cos_ref, sin_ref, ret_state_ref, ret_decay_ref)
    out(0)
    products(1)
    gate_piece(4)
    _retention_out(ret_scores, cv_ref, cg_ref, yc_ref)
    out(1)
    gate_piece(5)


def _gates_mixers(h, wg, bg, pf, pb, bound_consts, gain, cos, sin, layer, *, z_col, q_col, g_col, v_col,
                  cq_col, ck_col, cg_col, cv_col, tm, tn):
    s, d = h.shape
    n_gate = wg.shape[-1]
    n_col = n_gate // tn
    tb = tm // n_col
    assert tb % CHUNK == 0 and tb * n_col == tm and tn % GATE_PIECE == 0 and s % tm == 0
    level = jnp.asarray(_pair_level_matrix())
    assert B_WIDTH == C_V_WIDTH
    wide = lambda c: pl.BlockSpec((tb, B_WIDTH), lambda i, j: (i * n_col + j, c))
    narrow = lambda c: pl.BlockSpec((tb, C_QK_WIDTH), lambda i, j: (i * n_col + j, c))
    table = pl.BlockSpec((tb, LANES), lambda i, j: (i * n_col + j, 0))
    row = pl.BlockSpec((None, 1, B_WIDTH), lambda i, j: (layer, 0, 0))
    seq_out = pl.BlockSpec((tb, B_WIDTH), lambda i, j: (i * n_col + j, 0))
    return pl.pallas_call(
        _gates_mixers_kernel,
        grid=(s // tm, n_col),
        in_specs=[pl.BlockSpec((tm, d), lambda i, j: (i, 0)),
                  pl.BlockSpec((None, d, tn), lambda i, j: (layer, 0, j)),
                  pl.BlockSpec((None, 1, tn), lambda i, j: (layer, 0, j)),
                  wide(z_col), wide(q_col), wide(v_col), wide(g_col), row, row, row,
                  pl.BlockSpec((None, 1, B_VAL_DIM), lambda i, j: (layer, 0, 0)),
                  pl.BlockSpec((CHUNK, CHUNK), lambda i, j: (0, 0)),
                  narrow(cq_col), narrow(ck_col), wide(cv_col), wide(cg_col), table, table],
        out_specs=[pl.BlockSpec((tm, tn), lambda i, j: (i, j)), seq_out, seq_out],
        out_shape=[jax.ShapeDtypeStruct((s, n_gate), F32), jax.ShapeDtypeStruct((s, B_WIDTH), BF16),
                   jax.ShapeDtypeStruct((s, C_V_WIDTH), BF16)],
        scratch_shapes=[pltpu.VMEM((B_HEADS, B_VAL_DIM, B_KEY_DIM), F32),
                        pltpu.VMEM((C_HEADS, C_QK_DIM, C_V_DIM), F32), pltpu.VMEM((C_HEADS, tb, tb), F32)],
        compiler_params=_params("arbitrary", "arbitrary"),
        name="gates_mixers",
    )(h, wg, bg, pf, pf, pb, pf, *bound_consts, gain, level, pf, pf, pb, pf, cos, sin)


def _hgrn_bound_consts(lb_logits):
    lb_cum = jnp.cumsum(jax.nn.softmax(lb_logits.astype(F32), axis=0), axis=0)
    lb = (lb_cum - lb_cum[0:1])[:, None, :]
    return jnp.log2(lb), jnp.log1p(-lb) * LOG2E, 1.0 - lb


C_LOG_GAMMA = [np.float32(np.log(1.0 - 2.0 ** (-5.0 - h))) for h in range(C_HEADS)]


def _retention_init(state_ref, decay_ref):
    t = decay_ref.shape[-1]
    state_ref[...] = jnp.zeros_like(state_ref)
    ti = lax.broadcasted_iota(jnp.int32, (t, t), 0)
    si = lax.broadcasted_iota(jnp.int32, (t, t), 1)
    rel = (ti - si).astype(F32)
    for h in range(C_HEADS):
        decay_ref[h] = jnp.where(rel >= 0.0, jnp.exp(C_LOG_GAMMA[h] * jnp.maximum(rel, 0.0)), 0.0)


def _retention_scores(q_ref, k_ref, v_ref, cos_ref, sin_ref, state_ref, decay_ref):
    t = q_ref.shape[0]
    cos = cos_ref[...]
    sin = sin_ref[...]
    lane = lax.broadcasted_iota(jnp.int32, (t, LANES), 1)
    first_half = (lane % C_QK_DIM) < (C_QK_DIM // 2)

    def rotary(x):
        swapped = jnp.where(first_half, pltpu.roll(x, LANES - C_QK_DIM // 2, 1),
                            pltpu.roll(x, C_QK_DIM // 2, 1))
        return x * cos + swapped * sin

    pos = lax.broadcasted_iota(jnp.int32, (t, C_QK_DIM), 0).astype(F32)
    partial_out = []
    for pair in range(C_HEADS // 2):
        lanes = slice(pair * LANES, (pair + 1) * LANES)
        q2 = rotary(q_ref[:, lanes])
        k2 = rotary(k_ref[:, lanes]) * (C_QK_DIM ** -0.5)
        for sub in range(2):
            h = 2 * pair + sub
            qh = q2[:, sub * C_QK_DIM:(sub + 1) * C_QK_DIM]
            kh = k2[:, sub * C_QK_DIM:(sub + 1) * C_QK_DIM]
            vh = v_ref[:, h * C_V_DIM:(h + 1) * C_V_DIM]
            scores = lax.dot_general(qh.astype(BF16), kh.astype(BF16), NT_DIMS,
                                     preferred_element_type=F32) * decay_ref[h]
            state = state_ref[h]
            q_dec = qh * jnp.exp(C_LOG_GAMMA[h] * (pos + 1.0))
            o_past = jnp.dot(q_dec.astype(BF16), state.astype(BF16), preferred_element_type=F32)
            k_dec = kh * jnp.exp(C_LOG_GAMMA[h] * (t - 1.0 - pos))
            state_ref[h] = (np.float32(np.exp(C_LOG_GAMMA[h] * t)) * state
                            + lax.dot_general(k_dec.astype(BF16), vh, TN_DIMS, preferred_element_type=F32))
            partial_out.append((scores.astype(BF16), o_past))
    return partial_out


def _retention_out(partial_out, v_ref, g_ref, o_ref):
    for h, (scores, o_past) in enumerate(partial_out):
        cols = slice(h * C_V_DIM, (h + 1) * C_V_DIM)
        o = o_past + jnp.dot(scores, v_ref[:, cols], preferred_element_type=F32)
        o_ref[:, cols] = (_rms(o) * _silu(g_ref[:, cols])).astype(o_ref.dtype)


def _rotary_tables(seq):
    half = C_QK_DIM // 2
    inv_freq = jnp.asarray(1.0 / ROPE_BASE ** np.linspace(0.0, 1.0, half), F32)
    ang = jnp.arange(seq).astype(F32)[:, None] * inv_freq[None, :]
    cos, sin = jnp.cos(ang), jnp.sin(ang)
    reps = LANES // C_QK_DIM
    return (jnp.tile(jnp.concatenate([cos, cos], axis=1), (1, reps)),
            jnp.tile(jnp.concatenate([-sin, sin], axis=1), (1, reps)))


def _merge_kernel(ya_ref, yb_ref, yc_ref, ga_ref, gb_ref, gc_ref, x_ref, wa_ref, wb_ref, wc_ref,
                  wo_ref, xo_ref):
    merged = ga_ref[...] * jnp.dot(ya_ref[...], wa_ref[...], preferred_element_type=F32)
    merged = merged + gb_ref[...] * jnp.dot(yb_ref[...], wb_ref[...], preferred_element_type=F32)
    merged = merged + gc_ref[...] * jnp.dot(yc_ref[...], wc_ref[...], preferred_element_type=F32)
    xo_ref[...] = x_ref[...] + jnp.dot(merged.astype(BF16), wo_ref[...], preferred_element_type=F32)


def _merge_out(ya, yb, yc, gates, x, wa, wb, wc, wo, layer, *, tm):
    s, d = x.shape
    assert s % tm == 0
    once = pl.Buffered(1)
    y_spec = pl.BlockSpec((tm, A_WIDTH), lambda i: (i, 0))
    gate_spec = lambda c: pl.BlockSpec((tm, d), lambda i: (i, c))
    w_spec = pl.BlockSpec((None, A_WIDTH, d), lambda i: (layer, 0, 0), pipeline_mode=once)
    return pl.pallas_call(
        _merge_kernel,
        grid=(s // tm,),
        in_specs=[y_spec, y_spec, y_spec, gate_spec(0), gate_spec(1), gate_spec(2),
                  pl.BlockSpec((tm, d), lambda i: (i, 0)),
                  w_spec, w_spec, w_spec,
                  pl.BlockSpec((None, d, d), lambda i: (layer, 0, 0), pipeline_mode=once)],
        out_specs=pl.BlockSpec((tm, d), lambda i: (i, 0)),
        out_shape=jax.ShapeDtypeStruct((s, d), F32),
        compiler_params=_params("parallel"),
        name="merge_out",
    )(ya, yb, yc, gates, gates, gates, x, wa, wb, wc, wo)


def _mlp_kernel(x_ref, g_ref, wu_ref, wd_ref, o_ref, h_ref):
    @pl.when(pl.program_id(1) == 0)
    def _():
        x = x_ref[...]
        h_ref[...] = _rms(x, g_ref[...]).astype(BF16)
        o_ref[...] = x

    up = jnp.maximum(jnp.dot(h_ref[...], wu_ref[...], preferred_element_type=F32), 0.0)
    o_ref[...] += jnp.dot((up * up).astype(BF16), wd_ref[...], preferred_element_type=F32)


def _mlp(x, g, wu, wd, layer, *, tm, tf):
    s, d = x.shape
    ff = wu.shape[-1]
    assert s % tm == 0 and ff % tf == 0
    return pl.pallas_call(
        _mlp_kernel,
        grid=(s // tm, ff // tf),
        in_specs=[pl.BlockSpec((tm, d), lambda i, f: (i, 0)),
                  pl.BlockSpec((None, 1, d), lambda i, f: (layer, 0, 0)),
                  pl.BlockSpec((None, d, tf), lambda i, f: (layer, 0, f)),
                  pl.BlockSpec((None, tf, d), lambda i, f: (layer, f, 0))],
        out_specs=pl.BlockSpec((tm, d), lambda i, f: (i, 0)),
        out_shape=jax.ShapeDtypeStruct((s, d), F32),
        scratch_shapes=[pltpu.VMEM((tm, d), BF16)],
        compiler_params=_params("parallel", "arbitrary"),
        name="mlp",
    )(x, g, wu, wd)


def _layer(x, layer, w, consts, *, tm_proj, tn_gates, tm_merge, tm_mlp, tf_mlp):
    pf, pb, h = _proj(x, w["norm_mix_g"], w["w_in"], w["qn_g"], w["kn_g"], layer, tm=tm_proj)
    gates, yb, yc = _gates_mixers(h, w["w_gate"], w["b_gate"], pf, pb, consts["hgrn_bounds"],
                                  w["hgrn_norm_g"], consts["cos"], consts["sin"], layer,
                                  z_col=0, q_col=1, g_col=2, v_col=3, cq_col=3072 // C_QK_WIDTH,
                                  ck_col=3584 // C_QK_WIDTH, cg_col=4096 // C_V_WIDTH, cv_col=4,
                                  tm=tm_proj, tn=tn_gates)
    heads_per_tile = PROJ_TILE // A_HEAD_DIM
    ya = _attention(pb, consts["rel_bias"], layer, q_col=0, k_col=heads_per_tile, v_col=2 * heads_per_tile)

    x_new = _merge_out(ya, yb, yc, gates, x, w["w_br_a"], w["w_br_b"], w["w_br_c"], w["w_out"], layer,
                       tm=tm_merge)
    return _mlp(x_new, w["norm_ffn_g"], w["w_up"], w["w_down"], layer, tm=tm_mlp, tf=tf_mlp)


def _block(x, norm_mix_g, w_in, qn_g, kn_g, rel_bias, lb_logits, hgrn_norm_g, w_gate, b_gate,
           w_br_a, w_br_b, w_br_c, w_out, norm_ffn_g, w_up, w_down, **tiles):
    bsz, seq, d = x.shape
    depth = w_in.shape[0]
    row = lambda t: t.astype(F32)[:, None, :]
    w = dict(norm_mix_g=row(norm_mix_g), w_in=w_in.astype(BF16), qn_g=row(qn_g), kn_g=row(kn_g),
             hgrn_norm_g=row(hgrn_norm_g), w_gate=w_gate.astype(BF16), b_gate=row(b_gate),
             w_br_a=w_br_a.astype(BF16), w_br_b=w_br_b.astype(BF16), w_br_c=w_br_c.astype(BF16),
             w_out=w_out.astype(BF16), norm_ffn_g=row(norm_ffn_g), w_up=w_up.astype(BF16),
             w_down=w_down.astype(BF16))
    cos, sin = _rotary_tables(seq)
    consts = dict(rel_bias=_rel_bias_window(rel_bias), hgrn_bounds=_hgrn_bound_consts(lb_logits),
                  cos=cos, sin=sin)
    rows = x.reshape(bsz * seq, d)
    outs = []
    for b in range(bsz):
        xb = rows[b * seq:(b + 1) * seq]
        for layer in range(depth):
            xb = _layer(xb, layer, w, consts, **tiles)
        outs.append(xb)
    return jnp.concatenate(outs, axis=0).reshape(bsz, seq, d)


def kernel(x, norm_mix_g, w_in, qn_g, kn_g, rel_bias, lb_logits, hgrn_norm_g, w_gate, b_gate,
           w_br_a, w_br_b, w_br_c, w_out, norm_ffn_g, w_up, w_down):
    return _block(x, norm_mix_g, w_in, qn_g, kn_g, rel_bias, lb_logits, hgrn_norm_g, w_gate, b_gate,
                  w_br_a, w_br_b, w_br_c, w_out, norm_ffn_g, w_up, w_down,
                  tm_proj=1024, tn_gates=768, tm_merge=256, tm_mlp=1024, tf_mlp=1024)
```

```python
import functools

import numpy as np
import jax
import jax.numpy as jnp
from jax import lax
from jax.experimental import pallas as pl
from jax.experimental.pallas import tpu as pltpu

D_MODEL = 2048
CHUNK = 64
EPS = 1e-6

A_HEADS = 8
A_HEAD_DIM = 128
A_WIDTH = A_HEADS * A_HEAD_DIM
A_PAST_CHUNKS = 8
REL_CLIP = 256

B_HEADS = 8
B_KEY_DIM = 128
B_VAL_DIM = 128
B_WIDTH = B_HEADS * B_KEY_DIM

C_HEADS = 8
C_QK_DIM = 64
C_V_DIM = 128
C_QK_WIDTH = C_HEADS * C_QK_DIM
C_V_WIDTH = C_HEADS * C_V_DIM
ROPE_BASE = 10000.0

N_BRANCH = 3

LANES = 128
VMEM_LIMIT = 60 * 1024 * 1024

BF16 = jnp.bfloat16
F32 = jnp.float32
NT_DIMS = (((1,), (1,)), ((), ()))
TN_DIMS = (((0,), (0,)), ((), ()))
LOG2E = float(np.log2(np.e))


def _params(*semantics):
    return pltpu.CompilerParams(dimension_semantics=semantics, vmem_limit_bytes=VMEM_LIMIT)


def _rms(t, gain=None):
    y = t * lax.rsqrt(jnp.mean(t * t, axis=-1, keepdims=True) + EPS)
    return y if gain is None else y * gain


def _silu(t):
    return t / (1.0 + jnp.exp(-t))


PROJ_TILE = 1024
F32_TILES = (3, 4, 6, 7, 9)
BF16_TILES = (0, 1, 5, 8, 2)
A_Q_SCALE = A_HEAD_DIM ** -0.5 * LOG2E


def _proj_tile_of(j):
    tiles = F32_TILES + BF16_TILES
    tile = jnp.int32(tiles[0])
    for n in range(1, len(tiles)):
        tile = tile + (j >= n).astype(jnp.int32) * (tiles[n] - tiles[n - 1])
    return tile


def _proj_kernel(x_ref, g_ref, w_ref, gq_ref, gk_ref, pf_ref, pb_ref, h_ref):
    j = pl.program_id(1)
    n_f32 = len(F32_TILES)

    @pl.when(j == 0)
    def _():
        h_ref[...] = _rms(x_ref[...], g_ref[...]).astype(BF16)

    def product():
        return jnp.dot(h_ref[...], w_ref[...], preferred_element_type=F32)

    def head_norm(gain_ref, scale):
        acc = product()
        for h in range(A_HEADS):
            cols = slice(h * A_HEAD_DIM, (h + 1) * A_HEAD_DIM)
            pb_ref[:, cols] = (_rms(acc[:, cols], gain_ref[...]) * scale).astype(BF16)

    @pl.when(j < n_f32)
    def _():
        pf_ref[...] = product()

    pl.when(j == n_f32)(functools.partial(head_norm, gq_ref, A_Q_SCALE))
    pl.when(j == n_f32 + 1)(functools.partial(head_norm, gk_ref, 1.0))

    @pl.when(j > n_f32 + 1)
    def _():
        pb_ref[...] = product().astype(BF16)


def _proj(x, g, w, gq, gk, layer, *, tm):
    s, d = x.shape
    n_f32, n_bf16 = len(F32_TILES), len(BF16_TILES)
    assert BF16_TILES[:2] == (0, 1) and PROJ_TILE == A_WIDTH and s % tm == 0
    gain_spec = pl.BlockSpec((None, 1, A_HEAD_DIM), lambda i, j: (layer, 0, 0))
    return pl.pallas_call(
        _proj_kernel,
        grid=(s // tm, n_f32 + n_bf16),
        in_specs=[
            pl.BlockSpec((tm, d), lambda i, j: (i, 0)),
            pl.BlockSpec((None, 1, d), lambda i, j: (layer, 0, 0)),
            pl.BlockSpec((None, d, PROJ_TILE), lambda i, j: (layer, 0, _proj_tile_of(j))),
            gain_spec, gain_spec,
        ],
        out_specs=[pl.BlockSpec((tm, PROJ_TILE), lambda i, j: (i, jnp.minimum(j, n_f32 - 1))),
                   pl.BlockSpec((tm, PROJ_TILE), lambda i, j: (i, jnp.maximum(j - n_f32, 0))),
                   pl.BlockSpec((tm, d), lambda i, j: (i, 0))],
        out_shape=[jax.ShapeDtypeStruct((s, n_f32 * PROJ_TILE), F32),
                   jax.ShapeDtypeStruct((s, n_bf16 * PROJ_TILE), BF16),
                   jax.ShapeDtypeStruct((s, d), BF16)],
        compiler_params=_params("arbitrary", "arbitrary"),
        name="proj",
    )(x, g, w, gq, gk)


A_GROUP = A_PAST_CHUNKS * CHUNK
A_SUB = 4 * CHUNK
A_SUB_WINDOW = A_SUB + A_PAST_CHUNKS * CHUNK
A_STEP = 8 * A_GROUP


def _attn_kernel(q_ref, k_ref, v_ref, bias_ref, o_ref, kbuf, vbuf):
    i = pl.program_id(1)

    @pl.when(i == 0)
    def _():
        kbuf[0:A_GROUP, :] = jnp.zeros((A_GROUP, A_HEAD_DIM), BF16)
        vbuf[0:A_GROUP, :] = jnp.zeros((A_GROUP, A_HEAD_DIM), BF16)

    @pl.when(i > 0)
    def _():
        kbuf[0:A_GROUP, :] = kbuf[A_STEP:, :]
        vbuf[0:A_GROUP, :] = vbuf[A_STEP:, :]

    kbuf[A_GROUP:, :] = k_ref[...]
    vbuf[A_GROUP:, :] = v_ref[...]

    def attend(first_step):
        subs = range(0, A_STEP, A_SUB)
        scores = []
        for lo in subs:
            s = lax.dot_general(q_ref[lo:lo + A_SUB, :], kbuf[lo:lo + A_SUB_WINDOW, :], NT_DIMS,
                                preferred_element_type=F32) + bias_ref[...]
            if first_step and lo < A_GROUP:
                col = lax.broadcasted_iota(jnp.int32, s.shape, 1)
                s = jnp.where(col >= A_GROUP - lo, s, -jnp.inf)
            scores.append(s)
        probs = []
        for s in scores:
            p = jnp.exp2(s - jnp.max(s, axis=-1, keepdims=True))
            probs.append((p.astype(BF16), jnp.sum(p, axis=-1, keepdims=True)))
        for lo, (p, l) in zip(subs, probs):
            o = jnp.dot(p, vbuf[lo:lo + A_SUB_WINDOW, :], preferred_element_type=F32)
            o_ref[lo:lo + A_SUB, :] = (o / l).astype(o_ref.dtype)

    pl.when(i == 0)(functools.partial(attend, True))
    pl.when(i > 0)(functools.partial(attend, False))


def _attention(pb, bias, layer, *, q_col, k_col, v_col):
    s = pb.shape[0]
    assert s % A_STEP == 0
    blk = (A_STEP, A_HEAD_DIM)
    cur = lambda c: (lambda h, i: (i, c + h))
    return pl.pallas_call(
        _attn_kernel,
        grid=(A_HEADS, s // blk[0]),
        in_specs=[
            pl.BlockSpec(blk, cur(q_col)),
            pl.BlockSpec(blk, cur(k_col)),
            pl.BlockSpec(blk, cur(v_col)),
            pl.BlockSpec((None, None, A_SUB, A_SUB_WINDOW), lambda h, i: (layer, h, 0, 0)),
        ],
        out_specs=pl.BlockSpec(blk, lambda h, i: (i, h)),
        out_shape=jax.ShapeDtypeStruct((s, A_WIDTH), BF16),
        scratch_shapes=[pltpu.VMEM((A_GROUP + A_STEP, A_HEAD_DIM), BF16),
                        pltpu.VMEM((A_GROUP + A_STEP, A_HEAD_DIM), BF16)],
        compiler_params=_params("arbitrary", "arbitrary"),
        name="attention",
    )(pb, pb, pb, bias)


def _rel_bias_window(rel_table):
    t = rel_table.astype(F32)
    lead = t.shape[:-1]
    past = A_PAST_CHUNKS * CHUNK
    n = A_SUB + A_SUB_WINDOW
    n_far = past + A_SUB - 1 - REL_CLIP
    p = jnp.concatenate([jnp.broadcast_to(t[..., 2 * REL_CLIP:], lead + (n_far,)),
                         t[..., ::-1][..., :n - 1 - n_far],
                         jnp.zeros(lead + (1,), F32)], axis=-1)
    v = jnp.roll(p, -(A_SUB - 1), axis=-1)
    rows = jnp.tile(v, A_SUB)[..., :A_SUB * (n - 1)].reshape(lead + (A_SUB, n - 1))
    bias = rows[..., :A_SUB_WINDOW]
    q_chunk = np.arange(A_SUB)[:, None] // CHUNK
    k_chunk = np.arange(A_SUB_WINDOW)[None, :] // CHUNK
    band = (k_chunk >= q_chunk) & (k_chunk <= q_chunk + A_PAST_CHUNKS)
    return jnp.where(band, bias * LOG2E, -jnp.inf)


B_LEVELS = (32, 16, 8, 4, 2, 1)


def _pair_level_matrix():
    t = np.arange(CHUNK)[:, None]
    s = np.arange(CHUNK)[None, :]
    lvl = np.full((CHUNK, CHUNK), -1, np.int32)
    lvl[t == s] = 0
    for n, h in enumerate(B_LEVELS):
        m = (t // (2 * h) == s // (2 * h)) & (t % (2 * h) >= h) & (s % (2 * h) < h)
        lvl[m] = n + 1
    return lvl


def _hgrn_chunk_prep(z, qin, log2lb, log2_1mlb, one_m_lb):
    row = lax.broadcasted_iota(jnp.int32, (CHUNK, B_KEY_DIM), 0)
    z2 = z * LOG2E
    e = jnp.exp2(-jnp.abs(z2))
    log2_sig = jnp.minimum(z2, 0.0) - jnp.log2(1.0 + e)
    bb = log2_1mlb + log2_sig
    lf = jnp.maximum(log2lb, bb) + jnp.log2(1.0 + jnp.exp2(-jnp.abs(log2lb - bb)))
    kb = (one_m_lb * (jnp.where(z >= 0.0, e, 1.0) / (1.0 + e))).astype(BF16)
    qb = _silu(qin).astype(BF16)

    b = lf
    for sh in (1, 2, 4):
        b = b + jnp.where(row >= sh, pltpu.roll(b, sh, 0), 0.0)
    for sh in (8, 16, 32):
        b = b + jnp.concatenate([jnp.zeros((sh, B_KEY_DIM), F32), b[:CHUNK - sh, :]], axis=0)

    operands = [(qb, kb)]
    for h in B_LEVELS:
        if h >= 8:
            pieces = []
            for blk in range(0, CHUNK, 2 * h):
                bound = jnp.broadcast_to(b[blk + h - 1:blk + h, :], (h, B_KEY_DIM))
                pieces += [bound - b[blk:blk + h, :], b[blk + h:blk + 2 * h, :] - bound]
            expo = jnp.concatenate(pieces, axis=0)
        elif h == 4:
            bound = jnp.concatenate(
                [jnp.broadcast_to(b[blk + h - 1:blk + h, :], (2 * h, B_KEY_DIM))
                 for blk in range(0, CHUNK, 2 * h)], axis=0)
            expo = -jnp.abs(b - bound)
        elif h == 2:
            r4 = row % 4
            expo = jnp.where(r4 == 0, pltpu.roll(lf, CHUNK - 1, 0),
                             jnp.where(r4 == 1, 0.0,
                                       jnp.where(r4 == 2, lf, lf + pltpu.roll(lf, 1, 0))))
        else:
            expo = jnp.where(row % 2 == 1, lf, 0.0)
        w = jnp.exp2(expo).astype(BF16)
        operands.append((qb * w, kb * w))

    b_last = b[CHUNK - 1:CHUNK, :]
    q_dec = qb * jnp.exp2(b).astype(BF16)
    k_dec = kb * jnp.exp2(b_last - b).astype(BF16)
    return operands, q_dec, k_dec, jnp.exp2(b_last)


def _hgrn_chunk_products(prep, v, level, state_t):
    operands, q_dec, k_dec, chunk_decay = prep
    attn = 0.0
    for n, (qw, kw) in enumerate(operands):
        attn = jnp.where(level == n, lax.dot_general(qw, kw, NT_DIMS, preferred_element_type=F32), attn)
    o_past = lax.dot_general(q_dec, state_t.astype(BF16), NT_DIMS, preferred_element_type=F32)
    new_state_t = chunk_decay * state_t + lax.dot_general(v, k_dec, TN_DIMS, preferred_element_type=F32)
    return attn.astype(BF16), o_past, new_state_t


def _hgrn_chunk_out(attn, o_past, v, g, gain):
    o = o_past + jnp.dot(attn, v, preferred_element_type=F32)
    return _rms(o, gain) * _silu(g)


GATE_PIECE = 256
GATE_ROW_SPLIT = 2


def _gates_mixers_kernel(h_ref, wg_ref, bg_ref, pf_ref, pb_ref, rot_ref, log2lb_ref, log2_1mlb_ref,
                         omlb_ref, gain_ref, level_ref,
                         gates_ref, yb_ref, yc_ref, state_ref, ret_state_ref, ret_decay_ref):
    @pl.when(jnp.logical_and(pl.program_id(0) == 0, pl.program_id(1) == 0))
    def _():
        state_ref[...] = jnp.zeros_like(state_ref)
        _retention_init(ret_state_ref, ret_decay_ref)

    z_ref, q_ref, g_ref, cq_ref, ck_ref, cg_ref = (
        pf_ref.at[:, lo:lo + width] for lo, width in
        ((0, B_WIDTH), (B_WIDTH, B_WIDTH), (2 * B_WIDTH, B_WIDTH), (3 * B_WIDTH, C_QK_WIDTH),
         (3 * B_WIDTH + C_QK_WIDTH, C_QK_WIDTH), (3 * B_WIDTH + 2 * C_QK_WIDTH, C_V_WIDTH)))
    v_ref, cv_ref = pb_ref.at[:, 0:B_WIDTH], pb_ref.at[:, B_WIDTH:B_WIDTH + C_V_WIDTH]
    cos_ref, sin_ref = rot_ref.at[:, 0:LANES], rot_ref.at[:, LANES:2 * LANES]
    level = level_ref[...]
    gain = gain_ref[...]
    head_cols = [slice(h * B_KEY_DIM, (h + 1) * B_KEY_DIM) for h in range(B_HEADS)]
    piece_rows = gates_ref.shape[0] // GATE_ROW_SPLIT

    def gate_piece(p):
        rows = slice((p % GATE_ROW_SPLIT) * piece_rows, (p % GATE_ROW_SPLIT + 1) * piece_rows)
        cols = slice((p // GATE_ROW_SPLIT) * GATE_PIECE, (p // GATE_ROW_SPLIT + 1) * GATE_PIECE)
        acc = jnp.dot(h_ref[rows, :], wg_ref[:, cols], preferred_element_type=F32) + bg_ref[:, cols]
        gates_ref[rows, cols] = jax.nn.sigmoid(acc)

    staged = {}

    def prep(c):
        rows = slice(c * CHUNK, (c + 1) * CHUNK)
        staged[c] = [_hgrn_chunk_prep(z_ref[rows, cols], q_ref[rows, cols], log2lb_ref[:, cols],
                                      log2_1mlb_ref[:, cols], omlb_ref[:, cols]) for cols in head_cols]

    def products(c):
        rows = slice(c * CHUNK, (c + 1) * CHUNK)
        for h, cols in enumerate(head_cols):
            attn, o_past, st = _hgrn_chunk_products(staged[c][h], v_ref[rows, cols], level, state_ref[h])
            state_ref[h] = st
            staged[c][h] = (attn, o_past)

    def out(c):
        rows = slice(c * CHUNK, (c + 1) * CHUNK)
        for (attn, o_past), cols in zip(staged[c], head_cols):
            y = _hgrn_chunk_out(attn, o_past, v_ref[rows, cols], g_ref[rows, cols], gain)
            yb_ref[rows, cols] = y.astype(yb_ref.dtype)

    n_chunks = z_ref.shape[0] // CHUNK
    n_pieces = GATE_ROW_SPLIT * gates_ref.shape[1] // GATE_PIECE
    assert n_chunks == 2 and n_pieces == 6
    prep(0)
    gate_piece(0)
    gate_piece(1)
    products(0)
    prep(1)
    gate_piece(2)
    gate_piece(3)
    ret_scores = _retention_scores(cq_ref, ck_ref, cv_ref, cos_ref, sin_ref, ret_state_ref, ret_decay_ref)
    out(0)
    products(1)
    gate_piece(4)
    _retention_out(ret_scores, cv_ref, cg_ref, yc_ref)
    out(1)
    gate_piece(5)


def _gates_mixers(h, wg, bg, pf, pb, bound_consts, gain, rot, layer, *, pb_col, tm, tn):
    s, d = h.shape
    n_gate = wg.shape[-1]
    n_col = n_gate // tn
    tb = tm // n_col
    assert tb % CHUNK == 0 and tb * n_col == tm and tn % GATE_PIECE == 0 and s % tm == 0
    level = jnp.asarray(_pair_level_matrix())
    assert B_WIDTH == C_V_WIDTH and pf.shape[1] == 3 * B_WIDTH + 2 * C_QK_WIDTH + C_V_WIDTH
    seq_in = lambda width, c: pl.BlockSpec((tb, width), lambda i, j: (i * n_col + j, c))
    row = pl.BlockSpec((None, 1, B_WIDTH), lambda i, j: (layer, 0, 0))
    seq_out = pl.BlockSpec((tb, B_WIDTH), lambda i, j: (i * n_col + j, 0))
    return pl.pallas_call(
        _gates_mixers_kernel,
        grid=(s // tm, n_col),
        in_specs=[pl.BlockSpec((tm, d), lambda i, j: (i, 0)),
                  pl.BlockSpec((None, d, tn), lambda i, j: (layer, 0, j)),
                  pl.BlockSpec((None, 1, tn), lambda i, j: (layer, 0, j)),
                  seq_in(pf.shape[1], 0), seq_in(B_WIDTH + C_V_WIDTH, pb_col), seq_in(2 * LANES, 0),
                  row, row, row,
                  pl.BlockSpec((None, 1, B_VAL_DIM), lambda i, j: (layer, 0, 0)),
                  pl.BlockSpec((CHUNK, CHUNK), lambda i, j: (0, 0))],
        out_specs=[pl.BlockSpec((tm, tn), lambda i, j: (i, j)), seq_out, seq_out],
        out_shape=[jax.ShapeDtypeStruct((s, n_gate), F32), jax.ShapeDtypeStruct((s, B_WIDTH), BF16),
                   jax.ShapeDtypeStruct((s, C_V_WIDTH), BF16)],
        scratch_shapes=[pltpu.VMEM((B_HEADS, B_VAL_DIM, B_KEY_DIM), F32),
                        pltpu.VMEM((C_HEADS, C_QK_DIM, C_V_DIM), F32), pltpu.VMEM((C_HEADS, tb, tb), F32)],
        compiler_params=_params("arbitrary", "arbitrary"),
        name="gates_mixers",
    )(h, wg, bg, pf, pb, rot, *bound_consts, gain, level)


def _hgrn_bound_consts(lb_logits):
    lb_cum = jnp.cumsum(jax.nn.softmax(lb_logits.astype(F32), axis=0), axis=0)
    lb = (lb_cum - lb_cum[0:1])[:, None, :]
    return jnp.log2(lb), jnp.log1p(-lb) * LOG2E, 1.0 - lb


C_LOG_GAMMA = [np.float32(np.log(1.0 - 2.0 ** (-5.0 - h))) for h in range(C_HEADS)]


def _retention_init(state_ref, decay_ref):
    t = decay_ref.shape[-1]
    state_ref[...] = jnp.zeros_like(state_ref)
    ti = lax.broadcasted_iota(jnp.int32, (t, t), 0)
    si = lax.broadcasted_iota(jnp.int32, (t, t), 1)
    rel = (ti - si).astype(F32)
    for h in range(C_HEADS):
        decay_ref[h] = jnp.where(rel >= 0.0, jnp.exp(C_LOG_GAMMA[h] * jnp.maximum(rel, 0.0)), 0.0)


def _retention_scores(q_ref, k_ref, v_ref, cos_ref, sin_ref, state_ref, decay_ref):
    t = q_ref.shape[0]
    cos = cos_ref[...]
    sin = sin_ref[...]
    lane = lax.broadcasted_iota(jnp.int32, (t, LANES), 1)
    first_half = (lane % C_QK_DIM) < (C_QK_DIM // 2)

    def rotary(x):
        swapped = jnp.where(first_half, pltpu.roll(x, LANES - C_QK_DIM // 2, 1),
                            pltpu.roll(x, C_QK_DIM // 2, 1))
        return x * cos + swapped * sin

    pos = lax.broadcasted_iota(jnp.int32, (t, C_QK_DIM), 0).astype(F32)
    partial_out = []
    for pair in range(C_HEADS // 2):
        lanes = slice(pair * LANES, (pair + 1) * LANES)
        q2 = rotary(q_ref[:, lanes])
        k2 = rotary(k_ref[:, lanes]) * (C_QK_DIM ** -0.5)
        for sub in range(2):
            h = 2 * pair + sub
            qh = q2[:, sub * C_QK_DIM:(sub + 1) * C_QK_DIM]
            kh = k2[:, sub * C_QK_DIM:(sub + 1) * C_QK_DIM]
            vh = v_ref[:, h * C_V_DIM:(h + 1) * C_V_DIM]
            scores = lax.dot_general(qh.astype(BF16), kh.astype(BF16), NT_DIMS,
                                     preferred_element_type=F32) * decay_ref[h]
            state = state_ref[h]
            q_dec = qh * jnp.exp(C_LOG_GAMMA[h] * (pos + 1.0))
            o_past = jnp.dot(q_dec.astype(BF16), state.astype(BF16), preferred_element_type=F32)
            k_dec = kh * jnp.exp(C_LOG_GAMMA[h] * (t - 1.0 - pos))
            state_ref[h] = (np.float32(np.exp(C_LOG_GAMMA[h] * t)) * state
                            + lax.dot_general(k_dec.astype(BF16), vh, TN_DIMS, preferred_element_type=F32))
            partial_out.append((scores.astype(BF16), o_past))
    return partial_out


def _retention_out(partial_out, v_ref, g_ref, o_ref):
    for h, (scores, o_past) in enumerate(partial_out):
        cols = slice(h * C_V_DIM, (h + 1) * C_V_DIM)
        o = o_past + jnp.dot(scores, v_ref[:, cols], preferred_element_type=F32)
        o_ref[:, cols] = (_rms(o) * _silu(g_ref[:, cols])).astype(o_ref.dtype)


def _rotary_tables(seq):
    half = C_QK_DIM // 2
    inv_freq = jnp.asarray(1.0 / ROPE_BASE ** np.linspace(0.0, 1.0, half), F32)
    ang = jnp.arange(seq).astype(F32)[:, None] * inv_freq[None, :]
    cos, sin = jnp.cos(ang), jnp.sin(ang)
    reps = LANES // C_QK_DIM
    return jnp.concatenate([jnp.tile(jnp.concatenate([cos, cos], axis=1), (1, reps)),
                            jnp.tile(jnp.concatenate([-sin, sin], axis=1), (1, reps))], axis=1)


def _merge_kernel(ya_ref, yb_ref, yc_ref, ga_ref, gb_ref, gc_ref, x_ref, wa_ref, wb_ref, wc_ref,
                  wo_ref, xo_ref):
    merged = ga_ref[...] * jnp.dot(ya_ref[...], wa_ref[...], preferred_element_type=F32)
    merged = merged + gb_ref[...] * jnp.dot(yb_ref[...], wb_ref[...], preferred_element_type=F32)
    merged = merged + gc_ref[...] * jnp.dot(yc_ref[...], wc_ref[...], preferred_element_type=F32)
    xo_ref[...] = x_ref[...] + jnp.dot(merged.astype(BF16), wo_ref[...], preferred_element_type=F32)


def _merge_out(ya, yb, yc, gates, x, wa, wb, wc, wo, layer, *, tm):
    s, d = x.shape
    assert s % tm == 0
    once = pl.Buffered(1)
    y_spec = pl.BlockSpec((tm, A_WIDTH), lambda i: (i, 0))
    gate_spec = lambda c: pl.BlockSpec((tm, d), lambda i: (i, c))
    w_spec = pl.BlockSpec((None, A_WIDTH, d), lambda i: (layer, 0, 0), pipeline_mode=once)
    return pl.pallas_call(
        _merge_kernel,
        grid=(s // tm,),
        in_specs=[y_spec, y_spec, y_spec, gate_spec(0), gate_spec(1), gate_spec(2),
                  pl.BlockSpec((tm, d), lambda i: (i, 0)),
                  w_spec, w_spec, w_spec,
                  pl.BlockSpec((None, d, d), lambda i: (layer, 0, 0), pipeline_mode=once)],
        out_specs=pl.BlockSpec((tm, d), lambda i: (i, 0)),
        out_shape=jax.ShapeDtypeStruct((s, d), F32),
        compiler_params=_params("parallel"),
        name="merge_out",
    )(ya, yb, yc, gates, gates, gates, x, wa, wb, wc, wo)


def _mlp_kernel(x_ref, g_ref, wu_ref, wd_ref, o_ref, h_ref):
    @pl.when(pl.program_id(1) == 0)
    def _():
        x = x_ref[...]
        h_ref[...] = _rms(x, g_ref[...]).astype(BF16)
        o_ref[...] = x

    up = jnp.maximum(jnp.dot(h_ref[...], wu_ref[...], preferred_element_type=F32), 0.0)
    o_ref[...] += jnp.dot((up * up).astype(BF16), wd_ref[...], preferred_element_type=F32)


def _mlp(x, g, wu, wd, layer, *, tm, tf):
    s, d = x.shape
    ff = wu.shape[-1]
    assert s % tm == 0 and ff % tf == 0
    return pl.pallas_call(
        _mlp_kernel,
        grid=(s // tm, ff // tf),
        in_specs=[pl.BlockSpec((tm, d), lambda i, f: (i, 0)),
                  pl.BlockSpec((None, 1, d), lambda i, f: (layer, 0, 0)),
                  pl.BlockSpec((None, d, tf), lambda i, f: (layer, 0, f)),
                  pl.BlockSpec((None, tf, d), lambda i, f: (layer, f, 0))],
        out_specs=pl.BlockSpec((tm, d), lambda i, f: (i, 0)),
        out_shape=jax.ShapeDtypeStruct((s, d), F32),
        scratch_shapes=[pltpu.VMEM((tm, d), BF16)],
        compiler_params=_params("parallel", "arbitrary"),
        name="mlp",
    )(x, g, wu, wd)


def _layer(x, layer, w, consts, *, tm_proj, tn_gates, tm_merge, tm_mlp, tf_mlp):
    pf, pb, h = _proj(x, w["norm_mix_g"], w["w_in"], w["qn_g"], w["kn_g"], layer, tm=tm_proj)
    gates, yb, yc = _gates_mixers(h, w["w_gate"], w["b_gate"], pf, pb, consts["hgrn_bounds"],
                                  w["hgrn_norm_g"], consts["rotary"], layer, pb_col=1, tm=tm_proj, tn=tn_gates)
    heads_per_tile = PROJ_TILE // A_HEAD_DIM
    ya = _attention(pb, consts["rel_bias"], layer, q_col=0, k_col=heads_per_tile, v_col=4 * heads_per_tile)

    x_new = _merge_out(ya, yb, yc, gates, x, w["w_br_a"], w["w_br_b"], w["w_br_c"], w["w_out"], layer,
                       tm=tm_merge)
    return _mlp(x_new, w["norm_ffn_g"], w["w_up"], w["w_down"], layer, tm=tm_mlp, tf=tf_mlp)


def _block(x, norm_mix_g, w_in, qn_g, kn_g, rel_bias, lb_logits, hgrn_norm_g, w_gate, b_gate,
           w_br_a, w_br_b, w_br_c, w_out, norm_ffn_g, w_up, w_down, **tiles):
    bsz, seq, d = x.shape
    depth = w_in.shape[0]
    row = lambda t: t.astype(F32)[:, None, :]
    w = dict(norm_mix_g=row(norm_mix_g), w_in=w_in.astype(BF16), qn_g=row(qn_g), kn_g=row(kn_g),
             hgrn_norm_g=row(hgrn_norm_g), w_gate=w_gate.astype(BF16), b_gate=row(b_gate),
             w_br_a=w_br_a.astype(BF16), w_br_b=w_br_b.astype(BF16), w_br_c=w_br_c.astype(BF16),
             w_out=w_out.astype(BF16), norm_ffn_g=row(norm_ffn_g), w_up=w_up.astype(BF16),
             w_down=w_down.astype(BF16))
    consts = dict(rel_bias=_rel_bias_window(rel_bias), hgrn_bounds=_hgrn_bound_consts(lb_logits),
                  rotary=_rotary_tables(seq))
    rows = x.reshape(bsz * seq, d)
    outs = []
    for b in range(bsz):
        xb = rows[b * seq:(b + 1) * seq]
        for layer in range(depth):
            xb = _layer(xb, layer, w, consts, **tiles)
        outs.append(xb)
    return jnp.concatenate(outs, axis=0).reshape(bsz, seq, d)


def kernel(x, norm_mix_g, w_in, qn_g, kn_g, rel_bias, lb_logits, hgrn_norm_g, w_gate, b_gate,
           w_br_a, w_br_b, w_br_c, w_out, norm_ffn_g, w_up, w_down):
    return _block(x, norm_mix_g, w_in, qn_g, kn_g, rel_bias, lb_logits, hgrn_norm_g, w_gate, b_gate,
                  w_br_a, w_br_b, w_br_c, w_out, norm_ffn_g, w_up, w_down,
                  tm_proj=1024, tn_gates=768, tm_merge=256, tm_mlp=1024, tf_mlp=1024)
```

```python
import functools

import numpy as np
import jax
import jax.numpy as jnp
from jax import lax
from jax.experimental import pallas as pl
from jax.experimental.pallas import tpu as pltpu

CHUNK = 64
EPS = 1e-6

A_HEADS = 8
A_HEAD_DIM = 128
A_WIDTH = A_HEADS * A_HEAD_DIM
A_PAST_CHUNKS = 8
REL_CLIP = 256

B_HEADS = 8
B_KEY_DIM = 128
B_VAL_DIM = 128
B_WIDTH = B_HEADS * B_KEY_DIM

C_HEADS = 8
C_QK_DIM = 64
C_V_DIM = 128
C_QK_WIDTH = C_HEADS * C_QK_DIM
C_V_WIDTH = C_HEADS * C_V_DIM
ROPE_BASE = 10000.0

LANES = 128
VMEM_LIMIT = 60 * 1024 * 1024

BF16 = jnp.bfloat16
F32 = jnp.float32
NT_DIMS = (((1,), (1,)), ((), ()))
TN_DIMS = (((0,), (0,)), ((), ()))
LOG2E = float(np.log2(np.e))


def _params(*semantics):
    return pltpu.CompilerParams(dimension_semantics=semantics, vmem_limit_bytes=VMEM_LIMIT)


def _rms(t, gain=None):
    y = t * lax.rsqrt(jnp.mean(t * t, axis=-1, keepdims=True) + EPS)
    return y if gain is None else y * gain


def _silu(t):
    return t / (1.0 + jnp.exp(-t))


PROJ_TILE = 1024
F32_TILES = (3, 4, 6, 7, 9)
BF16_TILES = (0, 1, 5, 8, 2)
A_Q_SCALE = A_HEAD_DIM ** -0.5 * LOG2E


def _proj_tile_of(j):
    tiles = F32_TILES + BF16_TILES
    tile = jnp.int32(tiles[0])
    for n in range(1, len(tiles)):
        tile = tile + (j >= n).astype(jnp.int32) * (tiles[n] - tiles[n - 1])
    return tile


def _proj_kernel(x_ref, g_ref, w_ref, gq_ref, gk_ref, pf_ref, pb_ref, h_ref):
    j = pl.program_id(1)
    n_f32 = len(F32_TILES)

    @pl.when(j == 0)
    def _():
        h_ref[...] = _rms(x_ref[...], g_ref[...]).astype(BF16)

    def product():
        return jnp.dot(h_ref[...], w_ref[...], preferred_element_type=F32)

    def head_norm(gain_ref, scale):
        acc = product()
        for h in range(A_HEADS):
            cols = slice(h * A_HEAD_DIM, (h + 1) * A_HEAD_DIM)
            pb_ref[:, cols] = (_rms(acc[:, cols], gain_ref[...]) * scale).astype(BF16)

    @pl.when(j < n_f32)
    def _():
        pf_ref[...] = product()

    pl.when(j == n_f32)(functools.partial(head_norm, gq_ref, A_Q_SCALE))
    pl.when(j == n_f32 + 1)(functools.partial(head_norm, gk_ref, 1.0))

    @pl.when(j > n_f32 + 1)
    def _():
        pb_ref[...] = product().astype(BF16)


def _proj(x, g, w, gq, gk, layer, *, tm):
    s, d = x.shape
    n_f32, n_bf16 = len(F32_TILES), len(BF16_TILES)
    assert BF16_TILES[:2] == (0, 1) and PROJ_TILE == A_WIDTH and s % tm == 0
    gain_spec = pl.BlockSpec((None, 1, A_HEAD_DIM), lambda i, j: (layer, 0, 0))
    return pl.pallas_call(
        _proj_kernel,
        grid=(s // tm, n_f32 + n_bf16),
        in_specs=[
            pl.BlockSpec((tm, d), lambda i, j: (i, 0)),
            pl.BlockSpec((None, 1, d), lambda i, j: (layer, 0, 0)),
            pl.BlockSpec((None, d, PROJ_TILE), lambda i, j: (layer, 0, _proj_tile_of(j))),
            gain_spec, gain_spec,
        ],
        out_specs=[pl.BlockSpec((tm, PROJ_TILE), lambda i, j: (i, jnp.minimum(j, n_f32 - 1))),
                   pl.BlockSpec((tm, PROJ_TILE), lambda i, j: (i, jnp.maximum(j - n_f32, 0))),
                   pl.BlockSpec((tm, d), lambda i, j: (i, 0))],
        out_shape=[jax.ShapeDtypeStruct((s, n_f32 * PROJ_TILE), F32),
                   jax.ShapeDtypeStruct((s, n_bf16 * PROJ_TILE), BF16),
                   jax.ShapeDtypeStruct((s, d), BF16)],
        compiler_params=_params("arbitrary", "arbitrary"),
        name="proj",
    )(x, g, w, gq, gk)


A_GROUP = A_PAST_CHUNKS * CHUNK
A_SUB = 4 * CHUNK
A_SUB_WINDOW = A_SUB + A_PAST_CHUNKS * CHUNK
A_STEP = 8 * A_GROUP


def _attn_kernel(q_ref, k_ref, v_ref, bias_ref, o_ref, kbuf, vbuf):
    i = pl.program_id(1)

    @pl.when(i == 0)
    def _():
        kbuf[0:A_GROUP, :] = jnp.zeros((A_GROUP, A_HEAD_DIM), BF16)
        vbuf[0:A_GROUP, :] = jnp.zeros((A_GROUP, A_HEAD_DIM), BF16)

    @pl.when(i > 0)
    def _():
        kbuf[0:A_GROUP, :] = kbuf[A_STEP:, :]
        vbuf[0:A_GROUP, :] = vbuf[A_STEP:, :]

    kbuf[A_GROUP:, :] = k_ref[...]
    vbuf[A_GROUP:, :] = v_ref[...]

    def attend(first_step):
        subs = range(0, A_STEP, A_SUB)
        scores = []
        for lo in subs:
            s = lax.dot_general(q_ref[lo:lo + A_SUB, :], kbuf[lo:lo + A_SUB_WINDOW, :], NT_DIMS,
                                preferred_element_type=F32) + bias_ref[...]
            if first_step and lo < A_GROUP:
                col = lax.broadcasted_iota(jnp.int32, s.shape, 1)
                s = jnp.where(col >= A_GROUP - lo, s, -jnp.inf)
            scores.append(s)
        probs = []
        for s in scores:
            p = jnp.exp2(s - jnp.max(s, axis=-1, keepdims=True))
            probs.append((p.astype(BF16), jnp.sum(p, axis=-1, keepdims=True)))
        for lo, (p, l) in zip(subs, probs):
            o = jnp.dot(p, vbuf[lo:lo + A_SUB_WINDOW, :], preferred_element_type=F32)
            o_ref[lo:lo + A_SUB, :] = (o / l).astype(o_ref.dtype)

    pl.when(i == 0)(functools.partial(attend, True))
    pl.when(i > 0)(functools.partial(attend, False))


def _attention(pb, bias, layer, *, q_col, k_col, v_col):
    s = pb.shape[0]
    assert s % A_STEP == 0
    blk = (A_STEP, A_HEAD_DIM)
    cur = lambda c: (lambda h, i: (i, c + h))
    return pl.pallas_call(
        _attn_kernel,
        grid=(A_HEADS, s // blk[0]),
        in_specs=[
            pl.BlockSpec(blk, cur(q_col)),
            pl.BlockSpec(blk, cur(k_col)),
            pl.BlockSpec(blk, cur(v_col)),
            pl.BlockSpec((None, None, A_SUB, A_SUB_WINDOW), lambda h, i: (layer, h, 0, 0)),
        ],
        out_specs=pl.BlockSpec(blk, lambda h, i: (i, h)),
        out_shape=jax.ShapeDtypeStruct((s, A_WIDTH), BF16),
        scratch_shapes=[pltpu.VMEM((A_GROUP + A_STEP, A_HEAD_DIM), BF16),
                        pltpu.VMEM((A_GROUP + A_STEP, A_HEAD_DIM), BF16)],
        compiler_params=_params("arbitrary", "arbitrary"),
        name="attention",
    )(pb, pb, pb, bias)


def _rel_bias_window(rel_table):
    t = rel_table.astype(F32)
    lead = t.shape[:-1]
    past = A_PAST_CHUNKS * CHUNK
    n = A_SUB + A_SUB_WINDOW
    n_far = past + A_SUB - 1 - REL_CLIP
    p = jnp.concatenate([jnp.broadcast_to(t[..., 2 * REL_CLIP:], lead + (n_far,)),
                         t[..., ::-1][..., :n - 1 - n_far],
                         jnp.zeros(lead + (1,), F32)], axis=-1)
    v = jnp.roll(p, -(A_SUB - 1), axis=-1)
    rows = jnp.tile(v, A_SUB)[..., :A_SUB * (n - 1)].reshape(lead + (A_SUB, n - 1))
    bias = rows[..., :A_SUB_WINDOW]
    q_chunk = np.arange(A_SUB)[:, None] // CHUNK
    k_chunk = np.arange(A_SUB_WINDOW)[None, :] // CHUNK
    band = (k_chunk >= q_chunk) & (k_chunk <= q_chunk + A_PAST_CHUNKS)
    return jnp.where(band, bias * LOG2E, -jnp.inf)


B_LEVELS = (32, 16, 8, 4, 2, 1)


def _pair_level_matrix():
    t = np.arange(CHUNK)[:, None]
    s = np.arange(CHUNK)[None, :]
    lvl = np.full((CHUNK, CHUNK), -1, np.int32)
    lvl[t == s] = 0
    for n, h in enumerate(B_LEVELS):
        m = (t // (2 * h) == s // (2 * h)) & (t % (2 * h) >= h) & (s % (2 * h) < h)
        lvl[m] = n + 1
    return lvl


def _hgrn_chunk_prep(z, qin, log2lb, log2_1mlb, one_m_lb):
    row = lax.broadcasted_iota(jnp.int32, (CHUNK, B_KEY_DIM), 0)
    z2 = z * LOG2E
    e = jnp.exp2(-jnp.abs(z2))
    log2_sig = jnp.minimum(z2, 0.0) - jnp.log2(1.0 + e)
    bb = log2_1mlb + log2_sig
    lf = jnp.maximum(log2lb, bb) + jnp.log2(1.0 + jnp.exp2(-jnp.abs(log2lb - bb)))
    kb = (one_m_lb * (jnp.where(z >= 0.0, e, 1.0) / (1.0 + e))).astype(BF16)
    qb = _silu(qin).astype(BF16)

    b = lf
    for sh in (1, 2, 4):
        b = b + jnp.where(row >= sh, pltpu.roll(b, sh, 0), 0.0)
    for sh in (8, 16, 32):
        b = b + jnp.concatenate([jnp.zeros((sh, B_KEY_DIM), F32), b[:CHUNK - sh, :]], axis=0)

    operands = [(qb, kb)]
    for h in B_LEVELS:
        if h >= 8:
            pieces = []
            for blk in range(0, CHUNK, 2 * h):
                bound = jnp.broadcast_to(b[blk + h - 1:blk + h, :], (h, B_KEY_DIM))
                pieces += [bound - b[blk:blk + h, :], b[blk + h:blk + 2 * h, :] - bound]
            expo = jnp.concatenate(pieces, axis=0)
        elif h == 4:
            bound = jnp.concatenate(
                [jnp.broadcast_to(b[blk + h - 1:blk + h, :], (2 * h, B_KEY_DIM))
                 for blk in range(0, CHUNK, 2 * h)], axis=0)
            expo = -jnp.abs(b - bound)
        elif h == 2:
            r4 = row % 4
            expo = jnp.where(r4 == 0, pltpu.roll(lf, CHUNK - 1, 0),
                             jnp.where(r4 == 1, 0.0,
                                       jnp.where(r4 == 2, lf, lf + pltpu.roll(lf, 1, 0))))
        else:
            expo = jnp.where(row % 2 == 1, lf, 0.0)
        w = jnp.exp2(expo).astype(BF16)
        operands.append((qb * w, kb * w))

    b_last = b[CHUNK - 1:CHUNK, :]
    q_dec = qb * jnp.exp2(b).astype(BF16)
    k_dec = kb * jnp.exp2(b_last - b).astype(BF16)
    return operands, q_dec, k_dec, jnp.exp2(b_last)


def _hgrn_chunk_products(prep, v, level, state_t):
    operands, q_dec, k_dec, chunk_decay = prep
    attn = 0.0
    for n, (qw, kw) in enumerate(operands):
        attn = jnp.where(level == n, lax.dot_general(qw, kw, NT_DIMS, preferred_element_type=F32), attn)
    o_past = lax.dot_general(q_dec, state_t.astype(BF16), NT_DIMS, preferred_element_type=F32)
    new_state_t = chunk_decay * state_t + lax.dot_general(v, k_dec, TN_DIMS, preferred_element_type=F32)
    return attn.astype(BF16), o_past, new_state_t


def _hgrn_chunk_out(attn, o_past, v, g, gain):
    o = o_past + jnp.dot(attn, v, preferred_element_type=F32)
    return _rms(o, gain) * _silu(g)


GATE_PIECE = 256
GATE_ROW_SPLIT = 2


def _gates_mixers_kernel(h_ref, wg_ref, bg_ref, pf_ref, pb_ref, rot_ref, log2lb_ref, log2_1mlb_ref,
                         omlb_ref, gain_ref, level_ref,
                         gates_ref, yb_ref, yc_ref, state_ref, ret_state_ref, ret_decay_ref):
    @pl.when(jnp.logical_and(pl.program_id(0) == 0, pl.program_id(1) == 0))
    def _():
        state_ref[...] = jnp.zeros_like(state_ref)
        _retention_init(ret_state_ref, ret_decay_ref)

    z_ref, q_ref, g_ref, cq_ref, ck_ref, cg_ref = (
        pf_ref.at[:, lo:lo + width] for lo, width in
        ((0, B_WIDTH), (B_WIDTH, B_WIDTH), (2 * B_WIDTH, B_WIDTH), (3 * B_WIDTH, C_QK_WIDTH),
         (3 * B_WIDTH + C_QK_WIDTH, C_QK_WIDTH), (3 * B_WIDTH + 2 * C_QK_WIDTH, C_V_WIDTH)))
    v_ref, cv_ref = pb_ref.at[:, 0:B_WIDTH], pb_ref.at[:, B_WIDTH:B_WIDTH + C_V_WIDTH]
    cos_ref, sin_ref = rot_ref.at[:, 0:LANES], rot_ref.at[:, LANES:2 * LANES]
    level = level_ref[...]
    gain = gain_ref[...]
    head_cols = [slice(h * B_KEY_DIM, (h + 1) * B_KEY_DIM) for h in range(B_HEADS)]
    piece_rows = gates_ref.shape[0] // GATE_ROW_SPLIT

    def gate_piece(p):
        rows = slice((p % GATE_ROW_SPLIT) * piece_rows, (p % GATE_ROW_SPLIT + 1) * piece_rows)
        cols = slice((p // GATE_ROW_SPLIT) * GATE_PIECE, (p // GATE_ROW_SPLIT + 1) * GATE_PIECE)
        acc = jnp.dot(h_ref[rows, :], wg_ref[:, cols], preferred_element_type=F32) + bg_ref[:, cols]
        gates_ref[rows, cols] = jax.nn.sigmoid(acc)

    staged = {}

    def prep(c):
        rows = slice(c * CHUNK, (c + 1) * CHUNK)
        staged[c] = [_hgrn_chunk_prep(z_ref[rows, cols], q_ref[rows, cols], log2lb_ref[:, cols],
                                      log2_1mlb_ref[:, cols], omlb_ref[:, cols]) for cols in head_cols]

    def products(c):
        rows = slice(c * CHUNK, (c + 1) * CHUNK)
        for h, cols in enumerate(head_cols):
            attn, o_past, st = _hgrn_chunk_products(staged[c][h], v_ref[rows, cols], level, state_ref[h])
            state_ref[h] = st
            staged[c][h] = (attn, o_past)

    def out(c):
        rows = slice(c * CHUNK, (c + 1) * CHUNK)
        for (attn, o_past), cols in zip(staged[c], head_cols):
            y = _hgrn_chunk_out(attn, o_past, v_ref[rows, cols], g_ref[rows, cols], gain)
            yb_ref[rows, cols] = y.astype(yb_ref.dtype)

    n_chunks = z_ref.shape[0] // CHUNK
    n_pieces = GATE_ROW_SPLIT * gates_ref.shape[1] // GATE_PIECE
    assert n_chunks == 2 and n_pieces == 6
    prep(0)
    gate_piece(0)
    gate_piece(1)
    products(0)
    prep(1)
    gate_piece(2)
    gate_piece(3)
    ret_scores = _retention_scores(cq_ref, ck_ref, cv_ref, cos_ref, sin_ref, ret_state_ref, ret_decay_ref)
    out(0)
    products(1)
    gate_piece(4)
    _retention_out(ret_scores, cv_ref, cg_ref, yc_ref)
    out(1)
    gate_piece(5)


def _gates_mixers(h, wg, bg, pf, pb, bound_consts, gain, rot, layer, *, pb_col, tm, tn):
    s, d = h.shape
    n_gate = wg.shape[-1]
    n_col = n_gate // tn
    tb = tm // n_col
    assert tb % CHUNK == 0 and tb * n_col == tm and tn % GATE_PIECE == 0 and s % tm == 0
    level = jnp.asarray(_pair_level_matrix())
    assert B_WIDTH == C_V_WIDTH and pf.shape[1] == 3 * B_WIDTH + 2 * C_QK_WIDTH + C_V_WIDTH
    seq_in = lambda width, c: pl.BlockSpec((tb, width), lambda i, j: (i * n_col + j, c))
    row = pl.BlockSpec((None, 1, B_WIDTH), lambda i, j: (layer, 0, 0))
    seq_out = pl.BlockSpec((tb, B_WIDTH), lambda i, j: (i * n_col + j, 0))
    return pl.pallas_call(
        _gates_mixers_kernel,
        grid=(s // tm, n_col),
        in_specs=[pl.BlockSpec((tm, d), lambda i, j: (i, 0)),
                  pl.BlockSpec((None, d, tn), lambda i, j: (layer, 0, j)),
                  pl.BlockSpec((None, 1, tn), lambda i, j: (layer, 0, j)),
                  seq_in(pf.shape[1], 0), seq_in(B_WIDTH + C_V_WIDTH, pb_col), seq_in(2 * LANES, 0),
                  row, row, row,
                  pl.BlockSpec((None, 1, B_VAL_DIM), lambda i, j: (layer, 0, 0)),
                  pl.BlockSpec((CHUNK, CHUNK), lambda i, j: (0, 0))],
        out_specs=[pl.BlockSpec((tm, tn), lambda i, j: (i, j)), seq_out, seq_out],
        out_shape=[jax.ShapeDtypeStruct((s, n_gate), F32), jax.ShapeDtypeStruct((s, B_WIDTH), BF16),
                   jax.ShapeDtypeStruct((s, C_V_WIDTH), BF16)],
        scratch_shapes=[pltpu.VMEM((B_HEADS, B_VAL_DIM, B_KEY_DIM), F32),
                        pltpu.VMEM((C_HEADS, C_QK_DIM, C_V_DIM), F32), pltpu.VMEM((C_HEADS, tb, tb), F32)],
        compiler_params=_params("arbitrary", "arbitrary"),
        name="gates_mixers",
    )(h, wg, bg, pf, pb, rot, *bound_consts, gain, level)


def _hgrn_bound_consts(lb_logits):
    lb_cum = jnp.cumsum(jax.nn.softmax(lb_logits.astype(F32), axis=0), axis=0)
    lb = (lb_cum - lb_cum[0:1])[:, None, :]
    return jnp.log2(lb), jnp.log1p(-lb) * LOG2E, 1.0 - lb


C_LOG_GAMMA = [np.float32(np.log(1.0 - 2.0 ** (-5.0 - h))) for h in range(C_HEADS)]


def _retention_init(state_ref, decay_ref):
    t = decay_ref.shape[-1]
    state_ref[...] = jnp.zeros_like(state_ref)
    ti = lax.broadcasted_iota(jnp.int32, (t, t), 0)
    si = lax.broadcasted_iota(jnp.int32, (t, t), 1)
    rel = (ti - si).astype(F32)
    for h in range(C_HEADS):
        decay_ref[h] = jnp.where(rel >= 0.0, jnp.exp(C_LOG_GAMMA[h] * jnp.maximum(rel, 0.0)), 0.0)


def _retention_scores(q_ref, k_ref, v_ref, cos_ref, sin_ref, state_ref, decay_ref):
    t = q_ref.shape[0]
    cos = cos_ref[...]
    sin = sin_ref[...]
    lane = lax.broadcasted_iota(jnp.int32, (t, LANES), 1)
    first_half = (lane % C_QK_DIM) < (C_QK_DIM // 2)

    def rotary(x):
        swapped = jnp.where(first_half, pltpu.roll(x, LANES - C_QK_DIM // 2, 1),
                            pltpu.roll(x, C_QK_DIM // 2, 1))
        return x * cos + swapped * sin

    pos = lax.broadcasted_iota(jnp.int32, (t, C_QK_DIM), 0).astype(F32)
    partial_out = []
    for pair in range(C_HEADS // 2):
        lanes = slice(pair * LANES, (pair + 1) * LANES)
        q2 = rotary(q_ref[:, lanes])
        k2 = rotary(k_ref[:, lanes]) * (C_QK_DIM ** -0.5)
        for sub in range(2):
            h = 2 * pair + sub
            qh = q2[:, sub * C_QK_DIM:(sub + 1) * C_QK_DIM]
            kh = k2[:, sub * C_QK_DIM:(sub + 1) * C_QK_DIM]
            vh = v_ref[:, h * C_V_DIM:(h + 1) * C_V_DIM]
            scores = lax.dot_general(qh.astype(BF16), kh.astype(BF16), NT_DIMS,
                                     preferred_element_type=F32) * decay_ref[h]
            state = state_ref[h]
            q_dec = qh * jnp.exp(C_LOG_GAMMA[h] * (pos + 1.0))
            o_past = jnp.dot(q_dec.astype(BF16), state.astype(BF16), preferred_element_type=F32)
            k_dec = kh * jnp.exp(C_LOG_GAMMA[h] * (t - 1.0 - pos))
            state_ref[h] = (np.float32(np.exp(C_LOG_GAMMA[h] * t)) * state
                            + lax.dot_general(k_dec.astype(BF16), vh, TN_DIMS, preferred_element_type=F32))
            partial_out.append((scores.astype(BF16), o_past))
    return partial_out


def _retention_out(partial_out, v_ref, g_ref, o_ref):
    for h, (scores, o_past) in enumerate(partial_out):
        cols = slice(h * C_V_DIM, (h + 1) * C_V_DIM)
        o = o_past + jnp.dot(scores, v_ref[:, cols], preferred_element_type=F32)
        o_ref[:, cols] = (_rms(o) * _silu(g_ref[:, cols])).astype(o_ref.dtype)


def _rotary_tables(seq):
    half = C_QK_DIM // 2
    inv_freq = jnp.asarray(1.0 / ROPE_BASE ** np.linspace(0.0, 1.0, half), F32)
    ang = jnp.arange(seq).astype(F32)[:, None] * inv_freq[None, :]
    cos, sin = jnp.cos(ang), jnp.sin(ang)
    reps = LANES // C_QK_DIM
    return jnp.concatenate([jnp.tile(jnp.concatenate([cos, cos], axis=1), (1, reps)),
                            jnp.tile(jnp.concatenate([-sin, sin], axis=1), (1, reps))], axis=1)


def _merge_kernel(ya_ref, yb_ref, yc_ref, ga_ref, gb_ref, gc_ref, x_ref, wa_ref, wb_ref, wc_ref,
                  wo_ref, xo_ref):
    merged = ga_ref[...] * jnp.dot(ya_ref[...], wa_ref[...], preferred_element_type=F32)
    merged = merged + gb_ref[...] * jnp.dot(yb_ref[...], wb_ref[...], preferred_element_type=F32)
    merged = merged + gc_ref[...] * jnp.dot(yc_ref[...], wc_ref[...], preferred_element_type=F32)
    xo_ref[...] = x_ref[...] + jnp.dot(merged.astype(BF16), wo_ref[...], preferred_element_type=F32)


def _merge_out(ya, yb, yc, gates, x, wa, wb, wc, wo, layer, *, tm):
    s, d = x.shape
    assert s % tm == 0
    once = pl.Buffered(1)
    y_spec = pl.BlockSpec((tm, A_WIDTH), lambda i: (i, 0))
    gate_spec = lambda c: pl.BlockSpec((tm, d), lambda i: (i, c))
    w_spec = pl.BlockSpec((None, A_WIDTH, d), lambda i: (layer, 0, 0), pipeline_mode=once)
    return pl.pallas_call(
        _merge_kernel,
        grid=(s // tm,),
        in_specs=[y_spec, y_spec, y_spec, gate_spec(0), gate_spec(1), gate_spec(2),
                  pl.BlockSpec((tm, d), lambda i: (i, 0)),
                  w_spec, w_spec, w_spec,
                  pl.BlockSpec((None, d, d), lambda i: (layer, 0, 0), pipeline_mode=once)],
        out_specs=pl.BlockSpec((tm, d), lambda i: (i, 0)),
        out_shape=jax.ShapeDtypeStruct((s, d), F32),
        compiler_params=_params("parallel"),
        name="merge_out",
    )(ya, yb, yc, gates, gates, gates, x, wa, wb, wc, wo)


def _mlp_kernel(x_ref, g_ref, wu_ref, wd_ref, o_ref, h_ref):
    @pl.when(pl.program_id(1) == 0)
    def _():
        x = x_ref[...]
        h_ref[...] = _rms(x, g_ref[...]).astype(BF16)
        o_ref[...] = x

    up = jnp.maximum(jnp.dot(h_ref[...], wu_ref[...], preferred_element_type=F32), 0.0)
    o_ref[...] += jnp.dot((up * up).astype(BF16), wd_ref[...], preferred_element_type=F32)


def _mlp(x, g, wu, wd, layer, *, tm, tf):
    s, d = x.shape
    ff = wu.shape[-1]
    assert s % tm == 0 and ff % tf == 0
    return pl.pallas_call(
        _mlp_kernel,
        grid=(s // tm, ff // tf),
        in_specs=[pl.BlockSpec((tm, d), lambda i, f: (i, 0)),
                  pl.BlockSpec((None, 1, d), lambda i, f: (layer, 0, 0)),
                  pl.BlockSpec((None, d, tf), lambda i, f: (layer, 0, f)),
                  pl.BlockSpec((None, tf, d), lambda i, f: (layer, f, 0))],
        out_specs=pl.BlockSpec((tm, d), lambda i, f: (i, 0)),
        out_shape=jax.ShapeDtypeStruct((s, d), F32),
        scratch_shapes=[pltpu.VMEM((tm, d), BF16)],
        compiler_params=_params("parallel", "arbitrary"),
        name="mlp",
    )(x, g, wu, wd)


def _layer(x, layer, w, consts, *, tm_proj, tn_gates, tm_merge, tm_mlp, tf_mlp):
    pf, pb, h = _proj(x, w["norm_mix_g"], w["w_in"], w["qn_g"], w["kn_g"], layer, tm=tm_proj)
    gates, yb, yc = _gates_mixers(h, w["w_gate"], w["b_gate"], pf, pb, consts["hgrn_bounds"],
                                  w["hgrn_norm_g"], consts["rotary"], layer, pb_col=1, tm=tm_proj, tn=tn_gates)
    heads_per_tile = PROJ_TILE // A_HEAD_DIM
    ya = _attention(pb, consts["rel_bias"], layer, q_col=0, k_col=heads_per_tile, v_col=4 * heads_per_tile)

    x_new = _merge_out(ya, yb, yc, gates, x, w["w_br_a"], w["w_br_b"], w["w_br_c"], w["w_out"], layer,
                       tm=tm_merge)
    return _mlp(x_new, w["norm_ffn_g"], w["w_up"], w["w_down"], layer, tm=tm_mlp, tf=tf_mlp)


def _block(x, norm_mix_g, w_in, qn_g, kn_g, rel_bias, lb_logits, hgrn_norm_g, w_gate, b_gate,
           w_br_a, w_br_b, w_br_c, w_out, norm_ffn_g, w_up, w_down, **tiles):
    bsz, seq, d = x.shape
    depth = w_in.shape[0]
    row = lambda t: t.astype(F32)[:, None, :]
    w = dict(norm_mix_g=row(norm_mix_g), w_in=w_in.astype(BF16), qn_g=row(qn_g), kn_g=row(kn_g),
             hgrn_norm_g=row(hgrn_norm_g), w_gate=w_gate.astype(BF16), b_gate=row(b_gate),
             w_br_a=w_br_a.astype(BF16), w_br_b=w_br_b.astype(BF16), w_br_c=w_br_c.astype(BF16),
             w_out=w_out.astype(BF16), norm_ffn_g=row(norm_ffn_g), w_up=w_up.astype(BF16),
             w_down=w_down.astype(BF16))
    consts = dict(rel_bias=_rel_bias_window(rel_bias), hgrn_bounds=_hgrn_bound_consts(lb_logits),
                  rotary=_rotary_tables(seq))
    rows = x.reshape(bsz * seq, d)
    outs = []
    for b in range(bsz):
        xb = rows[b * seq:(b + 1) * seq]
        for layer in range(depth):
            xb = _layer(xb, layer, w, consts, **tiles)
        outs.append(xb)
    return jnp.concatenate(outs, axis=0).reshape(bsz, seq, d)


def kernel(x, norm_mix_g, w_in, qn_g, kn_g, rel_bias, lb_logits, hgrn_norm_g, w_gate, b_gate,
           w_br_a, w_br_b, w_br_c, w_out, norm_ffn_g, w_up, w_down):
    return _block(x, norm_mix_g, w_in, qn_g, kn_g, rel_bias, lb_logits, hgrn_norm_g, w_gate, b_gate,
                  w_br_a, w_br_b, w_br_c, w_out, norm_ffn_g, w_up, w_down,
                  tm_proj=1024, tn_gates=768, tm_merge=256, tm_mlp=1024, tf_mlp=1024)
```

```python
import functools

import numpy as np
import jax
import jax.numpy as jnp
from jax import lax
from jax.experimental import pallas as pl
from jax.experimental.pallas import tpu as pltpu

CHUNK = 64
EPS = 1e-6

A_HEADS = 8
A_HEAD_DIM = 128
A_WIDTH = A_HEADS * A_HEAD_DIM
A_PAST_CHUNKS = 8
REL_CLIP = 256

B_HEADS = 8
B_KEY_DIM = 128
B_VAL_DIM = 128
B_WIDTH = B_HEADS * B_KEY_DIM

C_HEADS = 8
C_QK_DIM = 64
C_V_DIM = 128
C_QK_WIDTH = C_HEADS * C_QK_DIM
C_V_WIDTH = C_HEADS * C_V_DIM
ROPE_BASE = 10000.0

LANES = 128
VMEM_LIMIT = 60 * 1024 * 1024

BF16 = jnp.bfloat16
F32 = jnp.float32
NT_DIMS = (((1,), (1,)), ((), ()))
TN_DIMS = (((0,), (0,)), ((), ()))
LOG2E = float(np.log2(np.e))


def _params(*semantics):
    return pltpu.CompilerParams(dimension_semantics=semantics, vmem_limit_bytes=VMEM_LIMIT)


def _rms(t, gain=None):
    y = t * lax.rsqrt(jnp.mean(t * t, axis=-1, keepdims=True) + EPS)
    return y if gain is None else y * gain


def _silu(t):
    return t / (1.0 + jnp.exp(-t))


PROJ_TILE = 1024
F32_TILES = (3, 4, 6, 7, 9)
BF16_TILES = (0, 1, 5, 8, 2)
A_Q_SCALE = A_HEAD_DIM ** -0.5 * LOG2E


def _proj_tile_of(j):
    tiles = F32_TILES + BF16_TILES
    tile = jnp.int32(tiles[0])
    for n in range(1, len(tiles)):
        tile = tile + (j >= n).astype(jnp.int32) * (tiles[n] - tiles[n - 1])
    return tile


def _proj_kernel(x_ref, g_ref, w_ref, gq_ref, gk_ref, pf_ref, pb_ref, h_ref):
    j = pl.program_id(1)
    n_f32 = len(F32_TILES)

    @pl.when(j == 0)
    def _():
        h_ref[...] = _rms(x_ref[...], g_ref[...]).astype(BF16)

    def product():
        return jnp.dot(h_ref[...], w_ref[...], preferred_element_type=F32)

    def head_norm(gain_ref, scale):
        acc = product()
        for h in range(A_HEADS):
            cols = slice(h * A_HEAD_DIM, (h + 1) * A_HEAD_DIM)
            pb_ref[:, cols] = (_rms(acc[:, cols], gain_ref[...]) * scale).astype(BF16)

    @pl.when(j < n_f32)
    def _():
        pf_ref[...] = product()

    pl.when(j == n_f32)(functools.partial(head_norm, gq_ref, A_Q_SCALE))
    pl.when(j == n_f32 + 1)(functools.partial(head_norm, gk_ref, 1.0))

    @pl.when(j > n_f32 + 1)
    def _():
        pb_ref[...] = product().astype(BF16)


def _proj(x, g, w, gq, gk, layer, *, tm):
    s, d = x.shape
    n_f32, n_bf16 = len(F32_TILES), len(BF16_TILES)
    assert BF16_TILES[:2] == (0, 1) and PROJ_TILE == A_WIDTH and s % tm == 0
    gain_spec = pl.BlockSpec((None, 1, A_HEAD_DIM), lambda i, j: (layer, 0, 0))
    return pl.pallas_call(
        _proj_kernel,
        grid=(s // tm, n_f32 + n_bf16),
        in_specs=[
            pl.BlockSpec((tm, d), lambda i, j: (i, 0)),
            pl.BlockSpec((None, 1, d), lambda i, j: (layer, 0, 0)),
            pl.BlockSpec((None, d, PROJ_TILE), lambda i, j: (layer, 0, _proj_tile_of(j))),
            gain_spec, gain_spec,
        ],
        out_specs=[pl.BlockSpec((tm, PROJ_TILE), lambda i, j: (i, jnp.minimum(j, n_f32 - 1))),
                   pl.BlockSpec((tm, PROJ_TILE), lambda i, j: (i, jnp.maximum(j - n_f32, 0))),
                   pl.BlockSpec((tm, d), lambda i, j: (i, 0))],
        out_shape=[jax.ShapeDtypeStruct((s, n_f32 * PROJ_TILE), F32),
                   jax.ShapeDtypeStruct((s, n_bf16 * PROJ_TILE), BF16),
                   jax.ShapeDtypeStruct((s, d), BF16)],
        compiler_params=_params("arbitrary", "arbitrary"),
        name="proj",
    )(x, g, w, gq, gk)


A_GROUP = A_PAST_CHUNKS * CHUNK
A_SUB = 2 * CHUNK
A_SUB_WINDOW = A_SUB + A_PAST_CHUNKS * CHUNK
A_STEP = 8 * A_GROUP


def _attn_kernel(q_ref, k_ref, v_ref, bias_ref, o_ref, kbuf, vbuf):
    i = pl.program_id(1)

    @pl.when(i == 0)
    def _():
        kbuf[0:A_GROUP, :] = jnp.zeros((A_GROUP, A_HEAD_DIM), BF16)
        vbuf[0:A_GROUP, :] = jnp.zeros((A_GROUP, A_HEAD_DIM), BF16)

    @pl.when(i > 0)
    def _():
        kbuf[0:A_GROUP, :] = kbuf[A_STEP:, :]
        vbuf[0:A_GROUP, :] = vbuf[A_STEP:, :]

    kbuf[A_GROUP:, :] = k_ref[...]
    vbuf[A_GROUP:, :] = v_ref[...]

    def attend(first_step):
        subs = range(0, A_STEP, A_SUB)
        scores = []
        for lo in subs:
            s = lax.dot_general(q_ref[lo:lo + A_SUB, :], kbuf[lo:lo + A_SUB_WINDOW, :], NT_DIMS,
                                preferred_element_type=F32) + bias_ref[...]
            if first_step and lo < A_GROUP:
                col = lax.broadcasted_iota(jnp.int32, s.shape, 1)
                s = jnp.where(col >= A_GROUP - lo, s, -jnp.inf)
            scores.append(s)
        probs = []
        for s in scores:
            p = jnp.exp2(s - jnp.max(s, axis=-1, keepdims=True))
            probs.append((p.astype(BF16), jnp.sum(p, axis=-1, keepdims=True)))
        for lo, (p, l) in zip(subs, probs):
            o = jnp.dot(p, vbuf[lo:lo + A_SUB_WINDOW, :], preferred_element_type=F32)
            o_ref[lo:lo + A_SUB, :] = (o / l).astype(o_ref.dtype)

    pl.when(i == 0)(functools.partial(attend, True))
    pl.when(i > 0)(functools.partial(attend, False))


def _attention(pb, bias, layer, *, q_col, k_col, v_col):
    s = pb.shape[0]
    assert s % A_STEP == 0
    blk = (A_STEP, A_HEAD_DIM)
    cur = lambda c: (lambda h, i: (i, c + h))
    return pl.pallas_call(
        _attn_kernel,
        grid=(A_HEADS, s // blk[0]),
        in_specs=[
            pl.BlockSpec(blk, cur(q_col)),
            pl.BlockSpec(blk, cur(k_col)),
            pl.BlockSpec(blk, cur(v_col)),
            pl.BlockSpec((None, None, A_SUB, A_SUB_WINDOW), lambda h, i: (layer, h, 0, 0)),
        ],
        out_specs=pl.BlockSpec(blk, lambda h, i: (i, h)),
        out_shape=jax.ShapeDtypeStruct((s, A_WIDTH), BF16),
        scratch_shapes=[pltpu.VMEM((A_GROUP + A_STEP, A_HEAD_DIM), BF16),
                        pltpu.VMEM((A_GROUP + A_STEP, A_HEAD_DIM), BF16)],
        compiler_params=_params("arbitrary", "arbitrary"),
        name="attention",
    )(pb, pb, pb, bias)


def _rel_bias_window(rel_table):
    t = rel_table.astype(F32)
    lead = t.shape[:-1]
    past = A_PAST_CHUNKS * CHUNK
    n = A_SUB + A_SUB_WINDOW
    n_far = past + A_SUB - 1 - REL_CLIP
    p = jnp.concatenate([jnp.broadcast_to(t[..., 2 * REL_CLIP:], lead + (n_far,)),
                         t[..., ::-1][..., :n - 1 - n_far],
                         jnp.zeros(lead + (1,), F32)], axis=-1)
    v = jnp.roll(p, -(A_SUB - 1), axis=-1)
    rows = jnp.tile(v, A_SUB)[..., :A_SUB * (n - 1)].reshape(lead + (A_SUB, n - 1))
    bias = rows[..., :A_SUB_WINDOW]
    q_chunk = np.arange(A_SUB)[:, None] // CHUNK
    k_chunk = np.arange(A_SUB_WINDOW)[None, :] // CHUNK
    band = (k_chunk >= q_chunk) & (k_chunk <= q_chunk + A_PAST_CHUNKS)
    return jnp.where(band, bias * LOG2E, -jnp.inf)


B_LEVELS = (32, 16, 8, 4, 2, 1)


def _pair_level_matrix():
    t = np.arange(CHUNK)[:, None]
    s = np.arange(CHUNK)[None, :]
    lvl = np.full((CHUNK, CHUNK), -1, np.int32)
    lvl[t == s] = 0
    for n, h in enumerate(B_LEVELS):
        m = (t // (2 * h) == s // (2 * h)) & (t % (2 * h) >= h) & (s % (2 * h) < h)
        lvl[m] = n + 1
    return lvl


def _hgrn_chunk_prep(z, qin, log2lb, log2_1mlb, one_m_lb):
    row = lax.broadcasted_iota(jnp.int32, (CHUNK, B_KEY_DIM), 0)
    z2 = z * LOG2E
    e = jnp.exp2(-jnp.abs(z2))
    log2_sig = jnp.minimum(z2, 0.0) - jnp.log2(1.0 + e)
    bb = log2_1mlb + log2_sig
    lf = jnp.maximum(log2lb, bb) + jnp.log2(1.0 + jnp.exp2(-jnp.abs(log2lb - bb)))
    kb = (one_m_lb * (jnp.where(z >= 0.0, e, 1.0) / (1.0 + e))).astype(BF16)
    qb = _silu(qin).astype(BF16)

    b = lf
    for sh in (1, 2, 4):
        b = b + jnp.where(row >= sh, pltpu.roll(b, sh, 0), 0.0)
    for sh in (8, 16, 32):
        b = b + jnp.concatenate([jnp.zeros((sh, B_KEY_DIM), F32), b[:CHUNK - sh, :]], axis=0)

    operands = [(qb, kb)]
    for h in B_LEVELS:
        if h >= 8:
            pieces = []
            for blk in range(0, CHUNK, 2 * h):
                bound = jnp.broadcast_to(b[blk + h - 1:blk + h, :], (h, B_KEY_DIM))
                pieces += [bound - b[blk:blk + h, :], b[blk + h:blk + 2 * h, :] - bound]
            expo = jnp.concatenate(pieces, axis=0)
        elif h == 4:
            bound = jnp.concatenate(
                [jnp.broadcast_to(b[blk + h - 1:blk + h, :], (2 * h, B_KEY_DIM))
                 for blk in range(0, CHUNK, 2 * h)], axis=0)
            expo = -jnp.abs(b - bound)
        elif h == 2:
            r4 = row % 4
            expo = jnp.where(r4 == 0, pltpu.roll(lf, CHUNK - 1, 0),
                             jnp.where(r4 == 1, 0.0,
                                       jnp.where(r4 == 2, lf, lf + pltpu.roll(lf, 1, 0))))
        else:
            expo = jnp.where(row % 2 == 1, lf, 0.0)
        w = jnp.exp2(expo).astype(BF16)
        operands.append((qb * w, kb * w))

    b_last = b[CHUNK - 1:CHUNK, :]
    q_dec = qb * jnp.exp2(b).astype(BF16)
    k_dec = kb * jnp.exp2(b_last - b).astype(BF16)
    return operands, q_dec, k_dec, jnp.exp2(b_last)


def _hgrn_chunk_products(prep, v, level, state_t):
    operands, q_dec, k_dec, chunk_decay = prep
    attn = 0.0
    for n, (qw, kw) in enumerate(operands):
        attn = jnp.where(level == n, lax.dot_general(qw, kw, NT_DIMS, preferred_element_type=F32), attn)
    o_past = lax.dot_general(q_dec, state_t.astype(BF16), NT_DIMS, preferred_element_type=F32)
    new_state_t = chunk_decay * state_t + lax.dot_general(v, k_dec, TN_DIMS, preferred_element_type=F32)
    return attn.astype(BF16), o_past, new_state_t


def _hgrn_chunk_out(attn, o_past, v, g, gain):
    o = o_past + jnp.dot(attn, v, preferred_element_type=F32)
    return _rms(o, gain) * _silu(g)


GATE_PIECE = 256
GATE_ROW_SPLIT = 2


def _gates_mixers_kernel(h_ref, wg_ref, bg_ref, pf_ref, pb_ref, rot_ref, log2lb_ref, log2_1mlb_ref,
                         omlb_ref, gain_ref, level_ref,
                         gates_ref, yb_ref, yc_ref, state_ref, ret_state_ref, ret_decay_ref):
    @pl.when(jnp.logical_and(pl.program_id(0) == 0, pl.program_id(1) == 0))
    def _():
        state_ref[...] = jnp.zeros_like(state_ref)
        _retention_init(ret_state_ref, ret_decay_ref)

    z_ref, q_ref, g_ref, cq_ref, ck_ref, cg_ref = (
        pf_ref.at[:, lo:lo + width] for lo, width in
        ((0, B_WIDTH), (B_WIDTH, B_WIDTH), (2 * B_WIDTH, B_WIDTH), (3 * B_WIDTH, C_QK_WIDTH),
         (3 * B_WIDTH + C_QK_WIDTH, C_QK_WIDTH), (3 * B_WIDTH + 2 * C_QK_WIDTH, C_V_WIDTH)))
    v_ref, cv_ref = pb_ref.at[:, 0:B_WIDTH], pb_ref.at[:, B_WIDTH:B_WIDTH + C_V_WIDTH]
    cos_ref, sin_ref = rot_ref.at[:, 0:LANES], rot_ref.at[:, LANES:2 * LANES]
    level = level_ref[...]
    gain = gain_ref[...]
    head_cols = [slice(h * B_KEY_DIM, (h + 1) * B_KEY_DIM) for h in range(B_HEADS)]
    piece_rows = gates_ref.shape[0] // GATE_ROW_SPLIT

    def gate_piece(p):
        rows = slice((p % GATE_ROW_SPLIT) * piece_rows, (p % GATE_ROW_SPLIT + 1) * piece_rows)
        cols = slice((p // GATE_ROW_SPLIT) * GATE_PIECE, (p // GATE_ROW_SPLIT + 1) * GATE_PIECE)
        acc = jnp.dot(h_ref[rows, :], wg_ref[:, cols], preferred_element_type=F32) + bg_ref[:, cols]
        gates_ref[rows, cols] = jax.nn.sigmoid(acc)

    staged = {}

    def prep(c):
        rows = slice(c * CHUNK, (c + 1) * CHUNK)
        staged[c] = [_hgrn_chunk_prep(z_ref[rows, cols], q_ref[rows, cols], log2lb_ref[:, cols],
                                      log2_1mlb_ref[:, cols], omlb_ref[:, cols]) for cols in head_cols]

    def products(c):
        rows = slice(c * CHUNK, (c + 1) * CHUNK)
        for h, cols in enumerate(head_cols):
            attn, o_past, st = _hgrn_chunk_products(staged[c][h], v_ref[rows, cols], level, state_ref[h])
            state_ref[h] = st
            staged[c][h] = (attn, o_past)

    def out(c):
        rows = slice(c * CHUNK, (c + 1) * CHUNK)
        for (attn, o_past), cols in zip(staged[c], head_cols):
            y = _hgrn_chunk_out(attn, o_past, v_ref[rows, cols], g_ref[rows, cols], gain)
            yb_ref[rows, cols] = y.astype(yb_ref.dtype)

    n_chunks = z_ref.shape[0] // CHUNK
    n_pieces = GATE_ROW_SPLIT * gates_ref.shape[1] // GATE_PIECE
    assert n_chunks == 2 and n_pieces == 6
    prep(0)
    gate_piece(0)
    gate_piece(1)
    products(0)
    prep(1)
    gate_piece(2)
    gate_piece(3)
    ret_scores = _retention_scores(cq_ref, ck_ref, cv_ref, cos_ref, sin_ref, ret_state_ref, ret_decay_ref)
    out(0)
    products(1)
    gate_piece(4)
    _retention_out(ret_scores, cv_ref, cg_ref, yc_ref)
    out(1)
    gate_piece(5)


def _gates_mixers(h, wg, bg, pf, pb, bound_consts, gain, rot, layer, *, pb_col, tm, tn):
    s, d = h.shape
    n_gate = wg.shape[-1]
    n_col = n_gate // tn
    tb = tm // n_col
    assert tb % CHUNK == 0 and tb * n_col == tm and tn % GATE_PIECE == 0 and s % tm == 0
    level = jnp.asarray(_pair_level_matrix())
    assert B_WIDTH == C_V_WIDTH and pf.shape[1] == 3 * B_WIDTH + 2 * C_QK_WIDTH + C_V_WIDTH
    seq_in = lambda width, c: pl.BlockSpec((tb, width), lambda i, j: (i * n_col + j, c))
    row = pl.BlockSpec((None, 1, B_WIDTH), lambda i, j: (layer, 0, 0))
    seq_out = pl.BlockSpec((tb, B_WIDTH), lambda i, j: (i * n_col + j, 0))
    return pl.pallas_call(
        _gates_mixers_kernel,
        grid=(s // tm, n_col),
        in_specs=[pl.BlockSpec((tm, d), lambda i, j: (i, 0)),
                  pl.BlockSpec((None, d, tn), lambda i, j: (layer, 0, j)),
                  pl.BlockSpec((None, 1, tn), lambda i, j: (layer, 0, j)),
                  seq_in(pf.shape[1], 0), seq_in(B_WIDTH + C_V_WIDTH, pb_col), seq_in(2 * LANES, 0),
                  row, row, row,
                  pl.BlockSpec((None, 1, B_VAL_DIM), lambda i, j: (layer, 0, 0)),
                  pl.BlockSpec((CHUNK, CHUNK), lambda i, j: (0, 0))],
        out_specs=[pl.BlockSpec((tm, tn), lambda i, j: (i, j)), seq_out, seq_out],
        out_shape=[jax.ShapeDtypeStruct((s, n_gate), F32), jax.ShapeDtypeStruct((s, B_WIDTH), BF16),
                   jax.ShapeDtypeStruct((s, C_V_WIDTH), BF16)],
        scratch_shapes=[pltpu.VMEM((B_HEADS, B_VAL_DIM, B_KEY_DIM), F32),
                        pltpu.VMEM((C_HEADS, C_QK_DIM, C_V_DIM), F32), pltpu.VMEM((C_HEADS, tb, tb), F32)],
        compiler_params=_params("arbitrary", "arbitrary"),
        name="gates_mixers",
    )(h, wg, bg, pf, pb, rot, *bound_consts, gain, level)


def _hgrn_bound_consts(lb_logits):
    lb_cum = jnp.cumsum(jax.nn.softmax(lb_logits.astype(F32), axis=0), axis=0)
    lb = (lb_cum - lb_cum[0:1])[:, None, :]
    return jnp.log2(lb), jnp.log1p(-lb) * LOG2E, 1.0 - lb


C_LOG_GAMMA = [np.float32(np.log(1.0 - 2.0 ** (-5.0 - h))) for h in range(C_HEADS)]


def _retention_init(state_ref, decay_ref):
    t = decay_ref.shape[-1]
    state_ref[...] = jnp.zeros_like(state_ref)
    ti = lax.broadcasted_iota(jnp.int32, (t, t), 0)
    si = lax.broadcasted_iota(jnp.int32, (t, t), 1)
    rel = (ti - si).astype(F32)
    for h in range(C_HEADS):
        decay_ref[h] = jnp.where(rel >= 0.0, jnp.exp(C_LOG_GAMMA[h] * jnp.maximum(rel, 0.0)), 0.0)


def _retention_scores(q_ref, k_ref, v_ref, cos_ref, sin_ref, state_ref, decay_ref):
    t = q_ref.shape[0]
    cos = cos_ref[...]
    sin = sin_ref[...]
    lane = lax.broadcasted_iota(jnp.int32, (t, LANES), 1)
    first_half = (lane % C_QK_DIM) < (C_QK_DIM // 2)

    def rotary(x):
        swapped = jnp.where(first_half, pltpu.roll(x, LANES - C_QK_DIM // 2, 1),
                            pltpu.roll(x, C_QK_DIM // 2, 1))
        return x * cos + swapped * sin

    pos = lax.broadcasted_iota(jnp.int32, (t, C_QK_DIM), 0).astype(F32)
    partial_out = []
    for pair in range(C_HEADS // 2):
        lanes = slice(pair * LANES, (pair + 1) * LANES)
        q2 = rotary(q_ref[:, lanes])
        k2 = rotary(k_ref[:, lanes]) * (C_QK_DIM ** -0.5)
        for sub in range(2):
            h = 2 * pair + sub
            qh = q2[:, sub * C_QK_DIM:(sub + 1) * C_QK_DIM]
            kh = k2[:, sub * C_QK_DIM:(sub + 1) * C_QK_DIM]
            vh = v_ref[:, h * C_V_DIM:(h + 1) * C_V_DIM]
            scores = lax.dot_general(qh.astype(BF16), kh.astype(BF16), NT_DIMS,
                                     preferred_element_type=F32) * decay_ref[h]
            state = state_ref[h]
            q_dec = qh * jnp.exp(C_LOG_GAMMA[h] * (pos + 1.0))
            o_past = jnp.dot(q_dec.astype(BF16), state.astype(BF16), preferred_element_type=F32)
            k_dec = kh * jnp.exp(C_LOG_GAMMA[h] * (t - 1.0 - pos))
            state_ref[h] = (np.float32(np.exp(C_LOG_GAMMA[h] * t)) * state
                            + lax.dot_general(k_dec.astype(BF16), vh, TN_DIMS, preferred_element_type=F32))
            partial_out.append((scores.astype(BF16), o_past))
    return partial_out


def _retention_out(partial_out, v_ref, g_ref, o_ref):
    for h, (scores, o_past) in enumerate(partial_out):
        cols = slice(h * C_V_DIM, (h + 1) * C_V_DIM)
        o = o_past + jnp.dot(scores, v_ref[:, cols], preferred_element_type=F32)
        o_ref[:, cols] = (_rms(o) * _silu(g_ref[:, cols])).astype(o_ref.dtype)


def _rotary_tables(seq):
    half = C_QK_DIM // 2
    inv_freq = jnp.asarray(1.0 / ROPE_BASE ** np.linspace(0.0, 1.0, half), F32)
    ang = jnp.arange(seq).astype(F32)[:, None] * inv_freq[None, :]
    cos, sin = jnp.cos(ang), jnp.sin(ang)
    reps = LANES // C_QK_DIM
    return jnp.concatenate([jnp.tile(jnp.concatenate([cos, cos], axis=1), (1, reps)),
                            jnp.tile(jnp.concatenate([-sin, sin], axis=1), (1, reps))], axis=1)


def _merge_kernel(ya_ref, yb_ref, yc_ref, ga_ref, gb_ref, gc_ref, x_ref, wa_ref, wb_ref, wc_ref,
                  wo_ref, xo_ref):
    merged = ga_ref[...] * jnp.dot(ya_ref[...], wa_ref[...], preferred_element_type=F32)
    merged = merged + gb_ref[...] * jnp.dot(yb_ref[...], wb_ref[...], preferred_element_type=F32)
    merged = merged + gc_ref[...] * jnp.dot(yc_ref[...], wc_ref[...], preferred_element_type=F32)
    xo_ref[...] = x_ref[...] + jnp.dot(merged.astype(BF16), wo_ref[...], preferred_element_type=F32)


def _merge_out(ya, yb, yc, gates, x, wa, wb, wc, wo, layer, *, tm):
    s, d = x.shape
    assert s % tm == 0
    once = pl.Buffered(1)
    y_spec = pl.BlockSpec((tm, A_WIDTH), lambda i: (i, 0))
    gate_spec = lambda c: pl.BlockSpec((tm, d), lambda i: (i, c))
    w_spec = pl.BlockSpec((None, A_WIDTH, d), lambda i: (layer, 0, 0), pipeline_mode=once)
    return pl.pallas_call(
        _merge_kernel,
        grid=(s // tm,),
        in_specs=[y_spec, y_spec, y_spec, gate_spec(0), gate_spec(1), gate_spec(2),
                  pl.BlockSpec((tm, d), lambda i: (i, 0)),
                  w_spec, w_spec, w_spec,
                  pl.BlockSpec((None, d, d), lambda i: (layer, 0, 0), pipeline_mode=once)],
        out_specs=pl.BlockSpec((tm, d), lambda i: (i, 0)),
        out_shape=jax.ShapeDtypeStruct((s, d), F32),
        compiler_params=_params("parallel"),
        name="merge_out",
    )(ya, yb, yc, gates, gates, gates, x, wa, wb, wc, wo)


def _mlp_kernel(x_ref, g_ref, wu_ref, wd_ref, o_ref, h_ref):
    @pl.when(pl.program_id(1) == 0)
    def _():
        x = x_ref[...]
        h_ref[...] = _rms(x, g_ref[...]).astype(BF16)
        o_ref[...] = x

    up = jnp.maximum(jnp.dot(h_ref[...], wu_ref[...], preferred_element_type=F32), 0.0)
    o_ref[...] += jnp.dot((up * up).astype(BF16), wd_ref[...], preferred_element_type=F32)


def _mlp(x, g, wu, wd, layer, *, tm, tf):
    s, d = x.shape
    ff = wu.shape[-1]
    assert s % tm == 0 and ff % tf == 0
    return pl.pallas_call(
        _mlp_kernel,
        grid=(s // tm, ff // tf),
        in_specs=[pl.BlockSpec((tm, d), lambda i, f: (i, 0)),
                  pl.BlockSpec((None, 1, d), lambda i, f: (layer, 0, 0)),
                  pl.BlockSpec((None, d, tf), lambda i, f: (layer, 0, f)),
                  pl.BlockSpec((None, tf, d), lambda i, f: (layer, f, 0))],
        out_specs=pl.BlockSpec((tm, d), lambda i, f: (i, 0)),
        out_shape=jax.ShapeDtypeStruct((s, d), F32),
        scratch_shapes=[pltpu.VMEM((tm, d), BF16)],
        compiler_params=_params("parallel", "arbitrary"),
        name="mlp",
    )(x, g, wu, wd)


def _layer(x, layer, w, consts, *, tm_proj, tn_gates, tm_merge, tm_mlp, tf_mlp):
    pf, pb, h = _proj(x, w["norm_mix_g"], w["w_in"], w["qn_g"], w["kn_g"], layer, tm=tm_proj)
    gates, yb, yc = _gates_mixers(h, w["w_gate"], w["b_gate"], pf, pb, consts["hgrn_bounds"],
                                  w["hgrn_norm_g"], consts["rotary"], layer, pb_col=1, tm=tm_proj, tn=tn_gates)
    heads_per_tile = PROJ_TILE // A_HEAD_DIM
    ya = _attention(pb, consts["rel_bias"], layer, q_col=0, k_col=heads_per_tile, v_col=4 * heads_per_tile)

    x_new = _merge_out(ya, yb, yc, gates, x, w["w_br_a"], w["w_br_b"], w["w_br_c"], w["w_out"], layer,
                       tm=tm_merge)
    return _mlp(x_new, w["norm_ffn_g"], w["w_up"], w["w_down"], layer, tm=tm_mlp, tf=tf_mlp)


def _block(x, norm_mix_g, w_in, qn_g, kn_g, rel_bias, lb_logits, hgrn_norm_g, w_gate, b_gate,
           w_br_a, w_br_b, w_br_c, w_out, norm_ffn_g, w_up, w_down, **tiles):
    bsz, seq, d = x.shape
    depth = w_in.shape[0]
    row = lambda t: t.astype(F32)[:, None, :]
    w = dict(norm_mix_g=row(norm_mix_g), w_in=w_in.astype(BF16), qn_g=row(qn_g), kn_g=row(kn_g),
             hgrn_norm_g=row(hgrn_norm_g), w_gate=w_gate.astype(BF16), b_gate=row(b_gate),
             w_br_a=w_br_a.astype(BF16), w_br_b=w_br_b.astype(BF16), w_br_c=w_br_c.astype(BF16),
             w_out=w_out.astype(BF16), norm_ffn_g=row(norm_ffn_g), w_up=w_up.astype(BF16),
             w_down=w_down.astype(BF16))
    consts = dict(rel_bias=_rel_bias_window(rel_bias), hgrn_bounds=_hgrn_bound_consts(lb_logits),
                  rotary=_rotary_tables(seq))
    rows = x.reshape(bsz * seq, d)
    outs = []
    for b in range(bsz):
        xb = rows[b * seq:(b + 1) * seq]
        for layer in range(depth):
            xb = _layer(xb, layer, w, consts, **tiles)
        outs.append(xb)
    return jnp.concatenate(outs, axis=0).reshape(bsz, seq, d)


def kernel(x, norm_mix_g, w_in, qn_g, kn_g, rel_bias, lb_logits, hgrn_norm_g, w_gate, b_gate,
           w_br_a, w_br_b, w_br_c, w_out, norm_ffn_g, w_up, w_down):
    return _block(x, norm_mix_g, w_in, qn_g, kn_g, rel_bias, lb_logits, hgrn_norm_g, w_gate, b_gate,
                  w_br_a, w_br_b, w_br_c, w_out, norm_ffn_g, w_up, w_down,
                  tm_proj=1024, tn_gates=768, tm_merge=256, tm_mlp=1024, tf_mlp=1024)
```
